```python
import math
import jax, jax.numpy as jnp
from jax import lax
import numpy as np

D_MODEL = 2048
BATCH = 4
SEQ = 2048
DEPTH = 2

EPS = 1e-6
NEG_INF = -1e30

SSM_WIDTH = D_MODEL // 2
SSM_GROUP = 16
SSM_GROUPS = SSM_WIDTH // SSM_GROUP
SSM_STATE = 64
DT_MIN = 1e-3
DT_MAX = 1e-1

SG_WIDTH = D_MODEL // 2
SG_HEADS = 8
SG_HEAD_DIM = SG_WIDTH // SG_HEADS
SG_CHUNK = 128

HEAD_DIM = 64
ATT_HEADS = D_MODEL // 128
ATT_KV_HEADS = ATT_HEADS // 8
GQA_GROUP = ATT_HEADS // ATT_KV_HEADS
ATT_WIDTH = ATT_HEADS * HEAD_DIM
KV_WIDTH = ATT_KV_HEADS * HEAD_DIM
WINDOW = 128
ATT_BLOCK = 128
ROT_DIM = HEAD_DIM // 4
ROPE_THETA = 500000.0

N_BRANCH = 3
IN_SIZES = (SSM_WIDTH, SSM_WIDTH, SG_WIDTH, SG_WIDTH, SG_WIDTH,
            ATT_WIDTH, KV_WIDTH, KV_WIDTH, ATT_WIDTH, N_BRANCH * D_MODEL)
D_IN = sum(IN_SIZES)
IN_OFFSETS = tuple(sum(IN_SIZES[:i + 1]) for i in range(len(IN_SIZES) - 1))

kernel_name = "hybrid_s5_gmlp_swa_gated_block"


def rmsnorm(x, w):
    xf = x.astype(jnp.float32)
    y = xf * lax.rsqrt(jnp.mean(xf * xf, axis=-1, keepdims=True) + EPS)
    return (y * w.astype(jnp.float32)).astype(x.dtype)


def layernorm(x, w, b):
    xf = x.astype(jnp.float32)
    mu = jnp.mean(xf, axis=-1, keepdims=True)
    var = jnp.mean(jnp.square(xf - mu), axis=-1, keepdims=True)
    y = (xf - mu) * lax.rsqrt(var + EPS)
    return (y * w.astype(jnp.float32) + b.astype(jnp.float32)).astype(x.dtype)


def partial_rope(t, pos):
    tf = t.astype(jnp.float32)
    half = ROT_DIM // 2
    inv_freq = ROPE_THETA ** (-jnp.arange(0, ROT_DIM, 2, dtype=jnp.float32) / ROT_DIM)
    ang = pos.astype(jnp.float32)[:, None] * inv_freq[None, :]
    cos = jnp.cos(ang)[None, :, None, :]
    sin = jnp.sin(ang)[None, :, None, :]
    t1 = tf[..., :half]
    t2 = tf[..., half:ROT_DIM]
    rot = jnp.concatenate([t1 * cos - t2 * sin, t2 * cos + t1 * sin, tf[..., ROT_DIM:]], axis=-1)
    return rot.astype(t.dtype)


def s5_mixer(u, a_re, a_im, log_dt, b_re, b_im, c_re, c_im, d, glu_w, glu_b):
    bsz, L, _ = u.shape
    uf = u.astype(jnp.float32).reshape(bsz, L, SSM_GROUPS, SSM_GROUP)
    lam = lax.complex(a_re.astype(jnp.float32), a_im.astype(jnp.float32))
    dt = jnp.exp(log_dt.astype(jnp.float32))[:, None]
    lam_bar = jnp.exp(lam * dt)
    b = lax.complex(b_re.astype(jnp.float32), b_im.astype(jnp.float32))
    b_bar = ((lam_bar - 1.0) / lam)[..., None] * b
    bu = jnp.einsum('blgc,gpc->blgp', uf.astype(jnp.complex64), b_bar)
    a = jnp.broadcast_to(lam_bar, bu.shape)

    def combine(e1, e2):
        a1, b1 = e1
        a2, b2 = e2
        return a1 * a2, a2 * b1 + b2

    _, states = lax.associative_scan(combine, (a, bu), axis=1)
    c = lax.complex(c_re.astype(jnp.float32), c_im.astype(jnp.float32))
    y = jnp.real(jnp.einsum('blgp,gcp->blgc', states, c))
    y = y + d.astype(jnp.float32).reshape(SSM_GROUPS, SSM_GROUP) * uf
    y = jax.nn.gelu(y.reshape(bsz, L, SSM_WIDTH)).astype(u.dtype)
    return y * jax.nn.sigmoid(y @ glu_w + glu_b)


def spatial_gating(u, v, ln_w, ln_b, w_s, b_s):
    bsz, L, _ = u.shape
    n = L // SG_CHUNK
    v = layernorm(v, ln_w, ln_b)
    vc = v.reshape(bsz, n, SG_CHUNK, SG_HEADS, SG_HEAD_DIM)
    causal = jnp.tril(jnp.ones((SG_CHUNK, SG_CHUNK), dtype=bool))
    w = jnp.where(causal[None], w_s, jnp.zeros_like(w_s))
    mixed = jnp.einsum('hts,bnshc->bnthc', w, vc) + b_s.T[:, :, None]
    return u * mixed.reshape(bsz, L, SG_WIDTH)


def sliding_window_attention(q, k, v, sinks):
    bsz, L, _ = q.shape
    n = L // ATT_BLOCK
    pos = jnp.arange(L)
    q = partial_rope(q.reshape(bsz, L, ATT_HEADS, HEAD_DIM), pos)
    k = partial_rope(k.reshape(bsz, L, ATT_KV_HEADS, HEAD_DIM), pos)
    v = v.reshape(bsz, L, ATT_KV_HEADS, HEAD_DIM)

    def banded(t):
        tp = jnp.pad(t, ((0, 0), (ATT_BLOCK, 0), (0, 0), (0, 0)))
        prev = tp[:, :L].reshape(bsz, n, ATT_BLOCK, ATT_KV_HEADS, HEAD_DIM)
        cur = t.reshape(bsz, n, ATT_BLOCK, ATT_KV_HEADS, HEAD_DIM)
        return jnp.concatenate([prev, cur], axis=2)

    kb = banded(k)
    vb = banded(v)
    qb = q.reshape(bsz, n, ATT_BLOCK, ATT_KV_HEADS, GQA_GROUP, HEAD_DIM)
    s = jnp.einsum('bnqkgd,bnskd->bnkgqs', qb, kb).astype(jnp.float32) * (HEAD_DIM ** -0.5)
    blk = jnp.arange(n)[:, None, None]
    qpos = blk * ATT_BLOCK + jnp.arange(ATT_BLOCK)[None, :, None]
    kpos = (blk - 1) * ATT_BLOCK + jnp.arange(2 * ATT_BLOCK)[None, None, :]
    diff = qpos - kpos
    allowed = (diff >= 0) & (diff < WINDOW) & (kpos >= 0)
    s = jnp.where(allowed[None, :, None, None], s, NEG_INF)
    sink = sinks.astype(jnp.float32).reshape(ATT_KV_HEADS, GQA_GROUP)[None, None, :, :, None, None]
    sink = jnp.broadcast_to(sink, s.shape[:-1] + (1,))
    p = jax.nn.softmax(jnp.concatenate([s, sink], axis=-1), axis=-1)[..., :-1]
    o = jnp.einsum('bnkgqs,bnskd->bnqkgd', p.astype(vb.dtype), vb)
    return o.reshape(bsz, L, ATT_WIDTH)


def setup_inputs(seed: int = 0) -> dict:
    key = jax.random.key(seed)
    ks = jax.random.split(key, 24)
    f32 = jnp.float32
    nrm = lambda k, shape, scale: jax.random.normal(k, shape, f32) * scale
    G, P, C = SSM_GROUPS, SSM_STATE, SSM_GROUP
    x = jax.random.normal(ks[0], (BATCH, SEQ, D_MODEL), f32)
    norm_w = 1.0 + nrm(ks[1], (DEPTH, D_MODEL), 0.02)
    w_in = nrm(ks[2], (DEPTH, D_MODEL, D_IN), D_MODEL ** -0.5)
    ssm_a_re = -0.5 + nrm(ks[3], (DEPTH, G, P), 0.01)
    ssm_a_im = math.pi * jnp.arange(P, dtype=f32)[None, None, :] + nrm(ks[4], (DEPTH, G, P), 0.01)
    ssm_log_dt = jax.random.uniform(ks[5], (DEPTH, G), f32, math.log(DT_MIN), math.log(DT_MAX))
    ssm_b_re = nrm(ks[6], (DEPTH, G, P, C), C ** -0.5)
    ssm_b_im = nrm(ks[7], (DEPTH, G, P, C), C ** -0.5)
    ssm_c_re = nrm(ks[8], (DEPTH, G, C, P), P ** -0.5)
    ssm_c_im = nrm(ks[9], (DEPTH, G, C, P), P ** -0.5)
    ssm_d = nrm(ks[10], (DEPTH, SSM_WIDTH), 1.0)
    ssm_glu_w = nrm(ks[11], (DEPTH, SSM_WIDTH, SSM_WIDTH), SSM_WIDTH ** -0.5)
    ssm_glu_b = nrm(ks[12], (DEPTH, SSM_WIDTH), 0.01)
    sg_ln_w = 1.0 + nrm(ks[13], (DEPTH, SG_WIDTH), 0.02)
    sg_ln_b = nrm(ks[14], (DEPTH, SG_WIDTH), 0.02)
    sg_w = nrm(ks[15], (DEPTH, SG_HEADS, SG_CHUNK, SG_CHUNK), SG_CHUNK ** -0.5)
    sg_b = 1.0 + nrm(ks[16], (DEPTH, SG_HEADS, SG_CHUNK), 0.02)
    attn_sinks = nrm(ks[17], (DEPTH, ATT_HEADS), 1.0)
    w_branch_a = nrm(ks[18], (DEPTH, SSM_WIDTH, D_MODEL), SSM_WIDTH ** -0.5)
    w_branch_b = nrm(ks[19], (DEPTH, SG_WIDTH, D_MODEL), SG_WIDTH ** -0.5)
    w_branch_c = nrm(ks[20], (DEPTH, ATT_WIDTH, D_MODEL), ATT_WIDTH ** -0.5)
    w_out = nrm(ks[21], (DEPTH, D_MODEL, D_MODEL), D_MODEL ** -0.5)
    final_norm_w = 1.0 + nrm(ks[22], (D_MODEL,), 0.02)
    return {"x": x, "norm_w": norm_w, "w_in": w_in,
            "ssm_a_re": ssm_a_re, "ssm_a_im": ssm_a_im, "ssm_log_dt": ssm_log_dt,
            "ssm_b_re": ssm_b_re, "ssm_b_im": ssm_b_im, "ssm_c_re": ssm_c_re, "ssm_c_im": ssm_c_im,
            "ssm_d": ssm_d, "ssm_glu_w": ssm_glu_w, "ssm_glu_b": ssm_glu_b,
            "sg_ln_w": sg_ln_w, "sg_ln_b": sg_ln_b, "sg_w": sg_w, "sg_b": sg_b,
            "attn_sinks": attn_sinks,
            "w_branch_a": w_branch_a, "w_branch_b": w_branch_b, "w_branch_c": w_branch_c,
            "w_out": w_out, "final_norm_w": final_norm_w}


def reference(x, norm_w, w_in, ssm_a_re, ssm_a_im, ssm_log_dt, ssm_b_re, ssm_b_im,
              ssm_c_re, ssm_c_im, ssm_d, ssm_glu_w, ssm_glu_b, sg_ln_w, sg_ln_b, sg_w, sg_b,
              attn_sinks, w_branch_a, w_branch_b, w_branch_c, w_out, final_norm_w):
    bsz, L, _ = x.shape
    for l in range(DEPTH):
        h = rmsnorm(x, norm_w[l])
        proj = h @ w_in[l]
        u_a, z_a, u_b, v_b, z_b, q, k, v, z_c, gates = jnp.split(proj, IN_OFFSETS, axis=-1)
        y_a = s5_mixer(u_a, ssm_a_re[l], ssm_a_im[l], ssm_log_dt[l], ssm_b_re[l], ssm_b_im[l],
                       ssm_c_re[l], ssm_c_im[l], ssm_d[l], ssm_glu_w[l], ssm_glu_b[l]) * jax.nn.silu(z_a)
        y_b = spatial_gating(jax.nn.gelu(u_b), jax.nn.gelu(v_b), sg_ln_w[l], sg_ln_b[l],
                             sg_w[l], sg_b[l]) * jax.nn.silu(z_b)
        y_c = sliding_window_attention(q, k, v, attn_sinks[l]) * jax.nn.silu(z_c)
        g = jax.nn.sigmoid(gates.reshape(bsz, L, N_BRANCH, D_MODEL))
        merged = (g[:, :, 0] * (y_a @ w_branch_a[l])
                  + g[:, :, 1] * (y_b @ w_branch_b[l])
                  + g[:, :, 2] * (y_c @ w_branch_c[l]))
        x = x + merged @ w_out[l]
    return rmsnorm(x, final_norm_w)
```

```python
import functools
import math

import jax
import jax.numpy as jnp
from jax import lax
from jax.experimental import pallas as pl
from jax.experimental.pallas import tpu as pltpu

f32 = jnp.float32
bf16 = jnp.bfloat16

D_MODEL = 2048
EPS = 1e-6
NEG_INF = -1e30

SSM_WIDTH = D_MODEL // 2
SSM_GROUP = 16
SSM_GROUPS = SSM_WIDTH // SSM_GROUP
SSM_STATE = 64

SG_WIDTH = D_MODEL // 2
SG_HEADS = 8
SG_CHUNK = 128

HEAD_DIM = 64
ATT_HEADS = D_MODEL // 128
ATT_KV_HEADS = ATT_HEADS // 8
ATT_WIDTH = ATT_HEADS * HEAD_DIM
KV_WIDTH = ATT_KV_HEADS * HEAD_DIM
ATT_BLOCK = 128
ROT_DIM = HEAD_DIM // 4
ROPE_THETA = 500000.0
N_BRANCH = 3

LANES = 128
SUBLANES = 8
VMEM_LIMIT_BYTES = 48 * 1024 * 1024

S5_GB = LANES // SSM_GROUP
S5_NGB = SSM_GROUPS // S5_GB
S5_HALF = S5_GB * SSM_STATE
S5_SW = 2 * S5_HALF


def _cparams(*sem):
    return pltpu.CompilerParams(dimension_semantics=sem, vmem_limit_bytes=VMEM_LIMIT_BYTES)


def _silu(x):
    return x * jax.nn.sigmoid(x)


def _rmsnorm_kernel(x_ref, w_ref, o_ref):
    x = x_ref[...]
    ms = jnp.mean(x * x, axis=-1, keepdims=True)
    o_ref[...] = (x * lax.rsqrt(ms + EPS) * w_ref[...]).astype(o_ref.dtype)


def _rmsnorm(x, w, out_dtype, tm=512):
    t, d = x.shape
    return pl.pallas_call(
        _rmsnorm_kernel,
        grid=(t // tm,),
        in_specs=[pl.BlockSpec((tm, d), lambda i: (i, 0)),
                  pl.BlockSpec((1, d), lambda i: (0, 0))],
        out_specs=pl.BlockSpec((tm, d), lambda i: (i, 0)),
        out_shape=jax.ShapeDtypeStruct((t, d), out_dtype),
        compiler_params=_cparams("parallel"),
        name="rmsnorm",
    )(x, w.reshape(1, d))


def _mm_kernel(h_ref, w_ref, o_ref):
    o_ref[...] = jnp.dot(h_ref[...], w_ref[...],
                         preferred_element_type=f32).astype(o_ref.dtype)


def _matmul(h, w, tm, tn, time_major_batches=None, name="matmul"):
    t, k = h.shape
    n = w.shape[1]
    nj = n // tn
    if time_major_batches is None:
        out_shape = (t, n)
        out_map = lambda i, j: (i, j)
    else:
        seq = t // time_major_batches
        nt = seq // tm
        out_shape = (seq, time_major_batches * n)
        out_map = lambda i, j: (i % nt, (i // nt) * nj + j)
    return pl.pallas_call(
        _mm_kernel,
        grid=(t // tm, nj),
        in_specs=[pl.BlockSpec((tm, k), lambda i, j: (i, 0)),
                  pl.BlockSpec((k, tn), lambda i, j: (0, j))],
        out_specs=pl.BlockSpec((tm, tn), out_map),
        out_shape=jax.ShapeDtypeStruct(out_shape, bf16),
        compiler_params=_cparams("parallel", "parallel"),
        name=name,
    )(h, w)


def _s5_kernel(u_ref, z_ref, w2_ref, lam2_ref, cm_ref, d_ref, gw_ref, gb_ref, o_ref,
               ubuf, ush_ref, bu_ref, st_ref, y_ref, carry_ref):
    i = pl.program_id(0)
    rows = u_ref.shape[0]
    shift = SUBLANES // 2

    @pl.when(i == 0)
    def _():
        carry_ref[...] = jnp.zeros_like(carry_ref)
        ubuf[0:SUBLANES, :] = jnp.zeros((SUBLANES, SSM_WIDTH), f32)

    @pl.when(i > 0)
    def _():
        ubuf[0:SUBLANES, :] = ubuf[rows:rows + SUBLANES, :]

    ubuf[SUBLANES:rows + SUBLANES, :] = u_ref[...].astype(f32)
    ush_ref[...] = ubuf[SUBLANES - shift:rows + SUBLANES - shift, :].astype(bf16)

    for g in range(S5_NGB):
        cs = slice(g * LANES, (g + 1) * LANES)
        lhs = jnp.concatenate([u_ref[:, cs], ush_ref[:, cs]], axis=1)
        bu_ref[...] = jnp.dot(lhs, w2_ref[g], preferred_element_type=f32)
        l_re = lam2_ref[g, :, :S5_HALF]
        l_im = lam2_ref[g, :, S5_HALF:]

        def body(r, s, l_re=l_re, l_im=l_im):
            s_re, s_im = s
            off = pl.multiple_of(r * SUBLANES, SUBLANES)
            b = bu_ref[pl.ds(off, SUBLANES), :]
            n_re = l_re * s_re - l_im * s_im + b[:, :S5_HALF]
            n_im = l_re * s_im + l_im * s_re + b[:, S5_HALF:]
            st_ref[pl.ds(off, SUBLANES), :S5_HALF] = n_re
            st_ref[pl.ds(off, SUBLANES), S5_HALF:] = n_im
            return n_re, n_im

        s0 = (carry_ref[g, :, :S5_HALF], carry_ref[g, :, S5_HALF:])
        s_re, s_im = lax.fori_loop(0, rows // SUBLANES, body, s0, unroll=2)
        carry_ref[g, :, :S5_HALF] = s_re
        carry_ref[g, :, S5_HALF:] = s_im
        y_ref[:, cs] = jnp.dot(st_ref[...].astype(bf16), cm_ref[g],
                               preferred_element_type=f32)

    y = y_ref[...] + d_ref[...] * ubuf[SUBLANES:rows + SUBLANES, :]
    y = jax.nn.gelu(y)
    gate = jnp.dot(y.astype(bf16), gw_ref[...], preferred_element_type=f32) + gb_ref[...]
    z = z_ref[...].astype(f32)
    o_ref[...] = (y * jax.nn.sigmoid(gate) * _silu(z)).astype(o_ref.dtype)


def _s5_params(a_re, a_im, log_dt, b_re, b_im, c_re, c_im):
    dt = jnp.exp(log_dt)[:, None]
    mag = jnp.exp(a_re * dt)
    lb_re = mag * jnp.cos(a_im * dt)
    lb_im = mag * jnp.sin(a_im * dt)
    den = a_re * a_re + a_im * a_im
    k_re = ((lb_re - 1.0) * a_re + lb_im * a_im) / den
    k_im = (lb_im * a_re - (lb_re - 1.0) * a_im) / den
    bb_re = k_re[..., None] * b_re - k_im[..., None] * b_im
    bb_im = k_re[..., None] * b_im + k_im[..., None] * b_re
    t_re = lb_re[..., None] * bb_re - lb_im[..., None] * bb_im
    t_im = lb_re[..., None] * bb_im + lb_im[..., None] * bb_re
    eye = jnp.eye(S5_GB, dtype=f32)

    def in_block(m):
        m = m.reshape(S5_NGB, S5_GB, SSM_STATE, SSM_GROUP)
        return jnp.einsum('agpc,gh->agchp', m, eye).reshape(S5_NGB, LANES, S5_HALF)

    w2 = jnp.concatenate([
        jnp.concatenate([in_block(bb_re), in_block(bb_im)], axis=2),
        jnp.concatenate([in_block(t_re), in_block(t_im)], axis=2)], axis=1)

    def out_block(m):
        m = m.reshape(S5_NGB, S5_GB, SSM_GROUP, SSM_STATE)
        return jnp.einsum('agcp,gh->agphc', m, eye).reshape(S5_NGB, S5_HALF, LANES)

    cm = jnp.concatenate([out_block(c_re), out_block(-c_im)], axis=1)
    l2_re = (lb_re * lb_re - lb_im * lb_im).reshape(S5_NGB, 1, S5_HALF)
    l2_im = (2.0 * lb_re * lb_im).reshape(S5_NGB, 1, S5_HALF)
    lam2 = jnp.broadcast_to(jnp.concatenate([l2_re, l2_im], axis=2),
                            (S5_NGB, SUBLANES, S5_SW))
    return w2.astype(bf16), lam2, cm.astype(bf16)


def _s5_mixer(uz, w2, lam2, cm, d, glu_w, glu_b, rows=512):
    n = uz.shape[0]
    w = SSM_WIDTH
    const3 = lambda i: (0, 0, 0)
    const2 = lambda i: (0, 0)
    return pl.pallas_call(
        _s5_kernel,
        grid=(n // rows,),
        in_specs=[pl.BlockSpec((rows, w), lambda i: (i, 0)),
                  pl.BlockSpec((rows, w), lambda i: (i, 1)),
                  pl.BlockSpec(w2.shape, const3),
                  pl.BlockSpec(lam2.shape, const3),
                  pl.BlockSpec(cm.shape, const3),
                  pl.BlockSpec((1, w), const2),
                  pl.BlockSpec((w, w), const2),
                  pl.BlockSpec((1, w), const2)],
        out_specs=pl.BlockSpec((rows, w), lambda i: (i, 0)),
        out_shape=jax.ShapeDtypeStruct((n, w), bf16),
        scratch_shapes=[pltpu.VMEM((rows + SUBLANES, w), f32),
                        pltpu.VMEM((rows, w), bf16),
                        pltpu.VMEM((rows, S5_SW), f32),
                        pltpu.VMEM((rows, S5_SW), f32),
                        pltpu.VMEM((rows, w), f32),
                        pltpu.VMEM((S5_NGB, SUBLANES, S5_SW), f32)],
        compiler_params=_cparams("arbitrary"),
        name="s5_mixer",
    )(uz, uz, w2, lam2, cm, d.reshape(1, w), glu_w, glu_b.reshape(1, w))


def _gmlp_kernel(u_ref, v_ref, z_ref, lnw_ref, lnb_ref, ws_ref, bias_ref, o_ref,
                 vn_ref, wm_ref):
    rows = u_ref.shape[0]
    v = jax.nn.gelu(v_ref[...].astype(f32))
    mu = jnp.mean(v, axis=-1, keepdims=True)
    vc = v - mu
    var = jnp.mean(vc * vc, axis=-1, keepdims=True)
    vn_ref[...] = (vc * lax.rsqrt(var + EPS) * lnw_ref[...] + lnb_ref[...]).astype(bf16)
    t_idx = lax.broadcasted_iota(jnp.int32, (SG_CHUNK, SG_CHUNK), 0)
    s_idx = lax.broadcasted_iota(jnp.int32, (SG_CHUNK, SG_CHUNK), 1)
    for h in range(SG_HEADS):
        wm_ref[h] = jnp.where(s_idx <= t_idx, ws_ref[h], 0.0).astype(bf16)
    for c in range(rows // SG_CHUNK):
        rs = slice(c * SG_CHUNK, (c + 1) * SG_CHUNK)
        for h in range(SG_HEADS):
            cs = slice(h * LANES, (h + 1) * LANES)
            mixed = jnp.dot(wm_ref[h], vn_ref[rs, cs], preferred_element_type=f32)
            mixed = mixed + bias_ref[:, cs]
            u = jax.nn.gelu(u_ref[rs, cs].astype(f32))
            z = z_ref[rs, cs].astype(f32)
            o_ref[rs, cs] = (u * mixed * _silu(z)).astype(o_ref.dtype)


def _gmlp_mixer(p, ln_w, ln_b, w_s, b_s, rows=512):
    t = p.shape[0]
    w = SG_WIDTH
    bias = jnp.repeat(b_s.T, LANES, axis=1)
    const2 = lambda i: (0, 0)
    return pl.pallas_call(
        _gmlp_kernel,
        grid=(t // rows,),
        in_specs=[pl.BlockSpec((rows, w), lambda i: (i, 0)),
                  pl.BlockSpec((rows, w), lambda i: (i, 1)),
                  pl.BlockSpec((rows, w), lambda i: (i, 2)),
                  pl.BlockSpec((1, w), const2),
                  pl.BlockSpec((1, w), const2),
                  pl.BlockSpec(w_s.shape, lambda i: (0, 0, 0)),
                  pl.BlockSpec((SG_CHUNK, w), const2)],
        out_specs=pl.BlockSpec((rows, w), lambda i: (i, 0)),
        out_shape=jax.ShapeDtypeStruct((t, w), bf16),
        scratch_shapes=[pltpu.VMEM((rows, w), bf16),
                        pltpu.VMEM((SG_HEADS, SG_CHUNK, SG_CHUNK), bf16)],
        compiler_params=_cparams("parallel"),
        name="gmlp_mixer",
    )(p, p, p, ln_w.reshape(1, w), ln_b.reshape(1, w), w_s, bias)


def _rope(x, cos, sin_up, sin_dn):
    half = ROT_DIM // 2
    return (x * cos + pltpu.roll(x, LANES - half, 1) * sin_up
            + pltpu.roll(x, half, 1) * sin_dn)


def _attn_kernel(sink_ref, q_ref, z_ref, kvp_ref, kvc_ref, rp_ref, rc_ref, o_ref):
    n = pl.program_id(1)
    blk = ATT_BLOCK
    cos_c, up_c, dn_c = rc_ref[0], rc_ref[1], rc_ref[2]
    cos_p, up_p, dn_p = rp_ref[0], rp_ref[1], rp_ref[2]

    k_all = jnp.concatenate([
        _rope(kvp_ref[:, :LANES].astype(f32), cos_p, up_p, dn_p),
        _rope(kvc_ref[:, :LANES].astype(f32), cos_c, up_c, dn_c)], axis=0)
    v_all = jnp.concatenate([kvp_ref[:, LANES:].astype(f32),
                             kvc_ref[:, LANES:].astype(f32)], axis=0)
    lane = lax.broadcasted_iota(jnp.int32, (2 * blk, LANES), 1)
    low = lane < HEAD_DIM
    k_sw = pltpu.roll(k_all, HEAD_DIM, 1)
    v_sw = pltpu.roll(v_all, HEAD_DIM, 1)
    k_pad = [[jnp.where(low, k_all, 0.0).astype(bf16), jnp.where(low, 0.0, k_sw).astype(bf16)],
             [jnp.where(low, k_sw, 0.0).astype(bf16), jnp.where(low, 0.0, k_all).astype(bf16)]]
    v_pad = [[jnp.where(low, v_all, 0.0).astype(bf16), jnp.where(low, 0.0, v_sw).astype(bf16)],
             [jnp.where(low, v_sw, 0.0).astype(bf16), jnp.where(low, 0.0, v_all).astype(bf16)]]

    qi = lax.broadcasted_iota(jnp.int32, (blk, 2 * blk), 0)
    ki = lax.broadcasted_iota(jnp.int32, (blk, 2 * blk), 1)
    allowed = (ki > qi) & (ki <= qi + blk) & ((ki >= blk) | (n > 0))

    scale = HEAD_DIM ** -0.5
    for j in range(ATT_HEADS // 2):
        cs = slice(j * LANES, (j + 1) * LANES)
        kh = (2 * j) // (ATT_HEADS // ATT_KV_HEADS)
        q = (_rope(q_ref[:, cs].astype(f32), cos_c, up_c, dn_c) * scale).astype(bf16)
        o = None
        for par in range(2):
            sink = sink_ref[2 * j + par]
            s = lax.dot_general(q, k_pad[kh][par], (((1,), (1,)), ((), ())),
                                preferred_element_type=f32)
            s = jnp.where(allowed, s, NEG_INF)
            m = jnp.maximum(jnp.max(s, axis=-1, keepdims=True), sink)
            p = jnp.exp(s - m)
            den = jnp.sum(p, axis=-1, keepdims=True) + jnp.exp(sink - m)
            pv = jnp.dot(p.astype(bf16), v_pad[kh][par], preferred_element_type=f32)
            pv = pv * (1.0 / den)
            o = pv if o is None else o + pv
        z = z_ref[:, cs].astype(f32)
        o_ref[:, cs] = (o * _silu(z)).astype(o_ref.dtype)


def _rope_tables(seq):
    half = ROT_DIM // 2
    inv_freq = ROPE_THETA ** (-jnp.arange(0, ROT_DIM, 2, dtype=f32) / ROT_DIM)
    ang = jnp.arange(seq, dtype=f32)[:, None] * inv_freq[None, :]
    cos, sin = jnp.cos(ang), jnp.sin(ang)
    ones = jnp.ones((seq, HEAD_DIM - ROT_DIM), f32)
    zeros = jnp.zeros((seq, HEAD_DIM - ROT_DIM), f32)
    zh = jnp.zeros((seq, half), f32)
    cos_t = jnp.concatenate([cos, cos, ones], axis=1)
    up_t = jnp.concatenate([-sin, zh, zeros], axis=1)
    dn_t = jnp.concatenate([zh, sin, zeros], axis=1)
    tab = jnp.stack([cos_t, up_t, dn_t])
    return jnp.concatenate([tab, tab], axis=2)


def _attn_mixer(p, kv, sinks, rope, bsz, seq):
    nb = seq // ATT_BLOCK
    w = ATT_WIDTH
    row = lambda b, n: b * nb + n
    prev = lambda b, n: b * nb + jnp.maximum(n - 1, 0)
    return pl.pallas_call(
        _attn_kernel,
        grid=(bsz, nb),
        in_specs=[pl.BlockSpec(memory_space=pltpu.SMEM),
                  pl.BlockSpec((ATT_BLOCK, w), lambda b, n: (row(b, n), 3)),
                  pl.BlockSpec((ATT_BLOCK, w), lambda b, n: (row(b, n), 4)),
                  pl.BlockSpec((ATT_BLOCK, 2 * KV_WIDTH), lambda b, n: (prev(b, n), 0)),
                  pl.BlockSpec((ATT_BLOCK, 2 * KV_WIDTH), lambda b, n: (row(b, n), 0)),
                  pl.BlockSpec((3, ATT_BLOCK, LANES), lambda b, n: (0, jnp.maximum(n - 1, 0), 0)),
                  pl.BlockSpec((3, ATT_BLOCK, LANES), lambda b, n: (0, n, 0))],
        out_specs=pl.BlockSpec((ATT_BLOCK, w), lambda b, n: (row(b, n), 0)),
        out_shape=jax.ShapeDtypeStruct((bsz * seq, w), bf16),
        compiler_params=_cparams("parallel", "parallel"),
        name="attn_mixer",
    )(sinks, p, p, kv, kv, rope, rope)


def _merge_kernel(ya_ref, yb_ref, yc_ref, g_ref, x_ref, wa_ref, wb_ref, wc_ref, wo_ref,
                  nw_ref, *out_refs, last, chunk):
    merged_ref = out_refs[-1]
    ys = (ya_ref, yb_ref, yc_ref)
    ws = (wa_ref, wb_ref, wc_ref)
    for c in range(D_MODEL // chunk):
        cs = slice(c * chunk, (c + 1) * chunk)
        acc = None
        for k in range(N_BRANCH):
            gate = jax.nn.sigmoid(g_ref[:, k * D_MODEL + c * chunk:k * D_MODEL + (c + 1) * chunk]
                                  .astype(f32))
            term = gate * jnp.dot(ys[k][...], ws[k][:, cs], preferred_element_type=f32)
            acc = term if acc is None else acc + term
        merged_ref[:, cs] = acc.astype(bf16)
    x_new = x_ref[...] + jnp.dot(merged_ref[...], wo_ref[...], preferred_element_type=f32)
    ms = jnp.mean(x_new * x_new, axis=-1, keepdims=True)
    normed = x_new * lax.rsqrt(ms + EPS) * nw_ref[...]
    if last:
        out_refs[0][...] = normed
    else:
        out_refs[0][...] = x_new
        out_refs[1][...] = normed.astype(bf16)


def _merge_out(ya_tm, yb, yc, gates, x, wa, wb, wc, wo, next_norm_w, bsz, seq, last, tm=256):
    t = bsz * seq
    nt = seq // tm
    d = D_MODEL
    row = lambda b, n: (b * nt + n, 0)
    const = lambda b, n: (0, 0)
    once = pl.Buffered(1)
    if last:
        out_shape = [jax.ShapeDtypeStruct((t, d), f32)]
        out_specs = [pl.BlockSpec((tm, d), row)]
    else:
        out_shape = [jax.ShapeDtypeStruct((t, d), f32), jax.ShapeDtypeStruct((t, d), bf16)]
        out_specs = [pl.BlockSpec((tm, d), row), pl.BlockSpec((tm, d), row)]
    return pl.pallas_call(
        functools.partial(_merge_kernel, last=last, chunk=512),
        grid=(bsz, nt),
        in_specs=[pl.BlockSpec((tm, SSM_WIDTH), lambda b, n: (n, b)),
                  pl.BlockSpec((tm, SG_WIDTH), row),
                  pl.BlockSpec((tm, ATT_WIDTH), row),
                  pl.BlockSpec((tm, N_BRANCH * d), row),
                  pl.BlockSpec((tm, d), row),
                  pl.BlockSpec(wa.shape, const, pipeline_mode=once),
                  pl.BlockSpec(wb.shape, const, pipeline_mode=once),
                  pl.BlockSpec(wc.shape, const, pipeline_mode=once),
                  pl.BlockSpec(wo.shape, const, pipeline_mode=once),
                  pl.BlockSpec((1, d), const)],
        out_specs=out_specs,
        out_shape=out_shape,
        scratch_shapes=[pltpu.VMEM((tm, d), bf16)],
        compiler_params=_cparams("parallel", "parallel"),
        name="merge_out",
    )(ya_tm, yb, yc, gates, x, wa, wb, wc, wo, next_norm_w.reshape(1, d))


def kernel(x, norm_w, w_in, ssm_a_re, ssm_a_im, ssm_log_dt, ssm_b_re, ssm_b_im, ssm_c_re, ssm_c_im, ssm_d, ssm_glu_w, ssm_glu_b, sg_ln_w, sg_ln_b, sg_w, sg_b, attn_sinks, w_branch_a, w_branch_b, w_branch_c, w_out, final_norm_w):
    bsz, seq, d = x.shape
    assert 2 * bsz == SUBLANES, "the S5 kernel packs two timesteps of all batches per sublane tile"
    depth = norm_w.shape[0]
    t = bsz * seq
    xf = x.reshape(t, d)
    rope = _rope_tables(seq)

    o_ub = 2 * SSM_WIDTH
    o_q = o_ub + 3 * SG_WIDTH
    o_k = o_q + ATT_WIDTH
    o_zc = o_k + 2 * KV_WIDTH
    o_g = o_zc + ATT_WIDTH

    h = _rmsnorm(xf, norm_w[0], bf16)
    out = None
    for l in range(depth):
        wl = w_in[l]
        w_a = wl[:, :o_ub].astype(bf16)
        w_bc = jnp.concatenate([wl[:, o_ub:o_k], wl[:, o_zc:o_g]], axis=1).astype(bf16)
        w_kv = wl[:, o_k:o_zc].astype(bf16)
        w_g = wl[:, o_g:].astype(bf16)

        uz = _matmul(h, w_a, 1024, 1024, time_major_batches=bsz, name="proj_s5")
        p_bc = _matmul(h, w_bc, 1024, 1024, name="proj_bc")
        kv = _matmul(h, w_kv, 1024, 2 * KV_WIDTH, name="proj_kv")
        gates = _matmul(h, w_g, 1024, 1024, name="proj_gates")

        w2, lam2, cm = _s5_params(ssm_a_re[l], ssm_a_im[l], ssm_log_dt[l], ssm_b_re[l],
                                  ssm_b_im[l], ssm_c_re[l], ssm_c_im[l])
        ya = _s5_mixer(uz.reshape(seq * bsz, 2 * SSM_WIDTH), w2, lam2, cm, ssm_d[l],
                       ssm_glu_w[l].astype(bf16), ssm_glu_b[l])
        yb = _gmlp_mixer(p_bc, sg_ln_w[l], sg_ln_b[l], sg_w[l], sg_b[l])
        yc = _attn_mixer(p_bc, kv, attn_sinks[l], rope, bsz, seq)

        last = l == depth - 1
        next_w = final_norm_w if last else norm_w[l + 1]
        res = _merge_out(ya.reshape(seq, bsz * SSM_WIDTH), yb, yc, gates, xf,
                         w_branch_a[l].astype(bf16), w_branch_b[l].astype(bf16),
                         w_branch_c[l].astype(bf16), w_out[l].astype(bf16), next_w,
                         bsz, seq, last)
        if last:
            out = res[0]
        else:
            xf, h = res
    return out.reshape(bsz, seq, d)
```

```python
import functools
import math

import jax
import jax.numpy as jnp
from jax import lax
from jax.experimental import pallas as pl
from jax.experimental.pallas import tpu as pltpu

f32 = jnp.float32
bf16 = jnp.bfloat16

D_MODEL = 2048
EPS = 1e-6
NEG_INF = -1e30

SSM_WIDTH = D_MODEL // 2
SSM_GROUP = 16
SSM_GROUPS = SSM_WIDTH // SSM_GROUP
SSM_STATE = 64

SG_WIDTH = D_MODEL // 2
SG_HEADS = 8
SG_CHUNK = 128

HEAD_DIM = 64
ATT_HEADS = D_MODEL // 128
ATT_KV_HEADS = ATT_HEADS // 8
ATT_WIDTH = ATT_HEADS * HEAD_DIM
KV_WIDTH = ATT_KV_HEADS * HEAD_DIM
ATT_BLOCK = 128
ROT_DIM = HEAD_DIM // 4
ROPE_THETA = 500000.0
N_BRANCH = 3

LANES = 128
SUBLANES = 8
VMEM_LIMIT_BYTES = 48 * 1024 * 1024

S5_GB = LANES // SSM_GROUP
S5_NGB = SSM_GROUPS // S5_GB
S5_HALF = S5_GB * SSM_STATE
S5_SW = 2 * S5_HALF


def _cparams(*sem):
    return pltpu.CompilerParams(dimension_semantics=sem, vmem_limit_bytes=VMEM_LIMIT_BYTES)


def _silu(x):
    return x * jax.nn.sigmoid(x)


def _rmsnorm_kernel(x_ref, w_ref, o_ref):
    x = x_ref[...]
    ms = jnp.mean(x * x, axis=-1, keepdims=True)
    o_ref[...] = (x * lax.rsqrt(ms + EPS) * w_ref[...]).astype(o_ref.dtype)


def _rmsnorm(x, w, out_dtype, tm=512):
    t, d = x.shape
    return pl.pallas_call(
        _rmsnorm_kernel,
        grid=(t // tm,),
        in_specs=[pl.BlockSpec((tm, d), lambda i: (i, 0)),
                  pl.BlockSpec((1, d), lambda i: (0, 0))],
        out_specs=pl.BlockSpec((tm, d), lambda i: (i, 0)),
        out_shape=jax.ShapeDtypeStruct((t, d), out_dtype),
        compiler_params=_cparams("parallel"),
        name="rmsnorm",
    )(x, w.reshape(1, d))


def _proj_kernel(offs_ref, h_ref, w_ref, o_ref, wbf_ref):
    del offs_ref
    @pl.when(pl.program_id(1) == 0)
    def _():
        wbf_ref[...] = w_ref[...].astype(bf16)

    o_ref[...] = jnp.dot(h_ref[...], wbf_ref[...],
                         preferred_element_type=f32).astype(o_ref.dtype)


def _project(h, w_all, layer, col_offsets, tn, tm=1024, time_major_batches=None, name="proj"):
    t, k = h.shape
    nj = len(col_offsets)
    ni = t // tm
    assert all(off % LANES == 0 for off in col_offsets)
    offs = jnp.asarray([off // LANES for off in col_offsets], jnp.int32)
    if time_major_batches is None:
        out_shape = (t, nj * tn)
        out_map = lambda j, i, offs_ref: (i, j)
    else:
        seq = t // time_major_batches
        nt = seq // tm
        out_shape = (seq, nj * time_major_batches * tn)
        out_map = lambda j, i, offs_ref: (i % nt, j * time_major_batches + i // nt)
    grid_spec = pltpu.PrefetchScalarGridSpec(
        num_scalar_prefetch=1,
        grid=(nj, ni),
        in_specs=[pl.BlockSpec((tm, k), lambda j, i, offs_ref: (i, 0)),
                  pl.BlockSpec((pl.Squeezed(), pl.Element(k), pl.Element(tn)),
                               lambda j, i, offs_ref: (layer, 0, offs_ref[j] * LANES))],
        out_specs=pl.BlockSpec((tm, tn), out_map),
        scratch_shapes=[pltpu.VMEM((k, tn), bf16)])
    return pl.pallas_call(
        _proj_kernel,
        grid_spec=grid_spec,
        out_shape=jax.ShapeDtypeStruct(out_shape, bf16),
        compiler_params=_cparams("parallel", "arbitrary"),
        name=name,
    )(offs, h, w_all)


def _s5_kernel(u_ref, z_ref, w2_ref, lam2_ref, cm_ref, d_ref, gw_ref, gb_ref, o_ref,
               ubuf, bu2_ref, st2_ref, y_ref, gwbf_ref, carry_ref):
    i = pl.program_id(0)
    steps = u_ref.shape[0]
    bsz = SUBLANES // 2
    rows = steps * bsz
    w = SSM_WIDTH

    @pl.when(i == 0)
    def _():
        carry_ref[...] = jnp.zeros_like(carry_ref)
        ubuf[:, 0:SUBLANES, :] = jnp.zeros((S5_NGB, SUBLANES, LANES), f32)
        gwbf_ref[...] = gw_ref[...].astype(bf16)

    @pl.when(i > 0)
    def _():
        ubuf[:, 0:SUBLANES, :] = ubuf[:, rows:rows + SUBLANES, :]

    for g in range(S5_NGB):
        for b in range(bsz):
            ubuf[g, pl.ds(SUBLANES + b, steps, stride=bsz), :] = (
                u_ref[:, b * w + g * LANES:b * w + (g + 1) * LANES].astype(f32))

    for g in range(S5_NGB):
        cs = slice(g * LANES, (g + 1) * LANES)
        lhs = jnp.concatenate(
            [ubuf[g, SUBLANES:rows + SUBLANES, :].astype(bf16),
             ubuf[g, SUBLANES - bsz:rows + SUBLANES - bsz, :].astype(bf16)], axis=1)
        bu_ref = bu2_ref.at[g % 2]
        st_ref = st2_ref.at[g % 2]
        bu_ref[...] = jnp.dot(lhs, w2_ref[g], preferred_element_type=f32)
        l_re = lam2_ref[g, :, :S5_HALF]
        l_im = lam2_ref[g, :, S5_HALF:]

        def body(r, s, l_re=l_re, l_im=l_im):
            s_re, s_im = s
            off = pl.multiple_of(r * SUBLANES, SUBLANES)
            b = bu_ref[pl.ds(off, SUBLANES), :]
            n_re = l_re * s_re - l_im * s_im + b[:, :S5_HALF]
            n_im = l_re * s_im + l_im * s_re + b[:, S5_HALF:]
            st_ref[pl.ds(off, SUBLANES), :S5_HALF] = n_re
            st_ref[pl.ds(off, SUBLANES), S5_HALF:] = n_im
            return n_re, n_im

        s0 = (carry_ref[g, :, :S5_HALF], carry_ref[g, :, S5_HALF:])
        s_re, s_im = lax.fori_loop(0, rows // SUBLANES, body, s0, unroll=True)
        carry_ref[g, :, :S5_HALF] = s_re
        carry_ref[g, :, S5_HALF:] = s_im
        y_g = jnp.dot(st_ref[...].astype(bf16), cm_ref[g], preferred_element_type=f32)
        y_ref[g] = y_g + d_ref[:, cs] * ubuf[g, SUBLANES:rows + SUBLANES, :]

    y = jax.nn.gelu(jnp.concatenate([y_ref[g] for g in range(S5_NGB)], axis=1))
    gate = jnp.dot(y.astype(bf16), gwbf_ref[...], preferred_element_type=f32) + gb_ref[...]
    y = y * jax.nn.sigmoid(gate)
    for g in range(S5_NGB):
        y_ref[g] = y[:, g * LANES:(g + 1) * LANES]
    for b in range(bsz):
        for g in range(S5_NGB):
            cs = slice(b * w + g * LANES, b * w + (g + 1) * LANES)
            z = z_ref[:, cs].astype(f32)
            o_ref[:, cs] = (y_ref[g, pl.ds(b, steps, stride=bsz), :]
                            * _silu(z)).astype(o_ref.dtype)


def _s5_params(a_re, a_im, log_dt, b_re, b_im, c_re, c_im):
    dt = jnp.exp(log_dt)[:, None]
    mag = jnp.exp(a_re * dt)
    lb_re = mag * jnp.cos(a_im * dt)
    lb_im = mag * jnp.sin(a_im * dt)
    den = a_re * a_re + a_im * a_im
    k_re = ((lb_re - 1.0) * a_re + lb_im * a_im) / den
    k_im = (lb_im * a_re - (lb_re - 1.0) * a_im) / den
    bb_re = k_re[..., None] * b_re - k_im[..., None] * b_im
    bb_im = k_re[..., None] * b_im + k_im[..., None] * b_re
    t_re = lb_re[..., None] * bb_re - lb_im[..., None] * bb_im
    t_im = lb_re[..., None] * bb_im + lb_im[..., None] * bb_re
    eye = jnp.eye(S5_GB, dtype=f32)

    def in_block(m):
        m = m.reshape(S5_NGB, S5_GB, SSM_STATE, SSM_GROUP)
        return jnp.einsum('agpc,gh->agchp', m, eye).reshape(S5_NGB, LANES, S5_HALF)

    w2 = jnp.concatenate([
        jnp.concatenate([in_block(bb_re), in_block(bb_im)], axis=2),
        jnp.concatenate([in_block(t_re), in_block(t_im)], axis=2)], axis=1)

    def out_block(m):
        m = m.reshape(S5_NGB, S5_GB, SSM_GROUP, SSM_STATE)
        return jnp.einsum('agcp,gh->agphc', m, eye).reshape(S5_NGB, S5_HALF, LANES)

    cm = jnp.concatenate([out_block(c_re), out_block(-c_im)], axis=1)
    l2_re = (lb_re * lb_re - lb_im * lb_im).reshape(S5_NGB, 1, S5_HALF)
    l2_im = (2.0 * lb_re * lb_im).reshape(S5_NGB, 1, S5_HALF)
    lam2 = jnp.broadcast_to(jnp.concatenate([l2_re, l2_im], axis=2),
                            (S5_NGB, SUBLANES, S5_SW))
    return w2.astype(bf16), lam2, cm.astype(bf16)


def _s5_mixer(uz, w2, lam2, cm, d, glu_w, layer, glu_b, bsz, steps=128):
    seq = uz.shape[0]
    w = SSM_WIDTH
    bw = bsz * w
    rows = steps * bsz
    const3 = lambda i: (0, 0, 0)
    const2 = lambda i: (0, 0)
    return pl.pallas_call(
        _s5_kernel,
        grid=(seq // steps,),
        in_specs=[pl.BlockSpec((steps, bw), lambda i: (i, 0)),
                  pl.BlockSpec((steps, bw), lambda i: (i, 1)),
                  pl.BlockSpec(w2.shape, const3),
                  pl.BlockSpec(lam2.shape, const3),
                  pl.BlockSpec(cm.shape, const3),
                  pl.BlockSpec((1, w), const2),
                  pl.BlockSpec((pl.Squeezed(), w, w), lambda i: (layer, 0, 0)),
                  pl.BlockSpec((1, w), const2)],
        out_specs=pl.BlockSpec((steps, bw), lambda i: (i, 0)),
        out_shape=jax.ShapeDtypeStruct((seq, bw), bf16),
        scratch_shapes=[pltpu.VMEM((S5_NGB, rows + SUBLANES, LANES), f32),
                        pltpu.VMEM((2, rows, S5_SW), f32),
                        pltpu.VMEM((2, rows, S5_SW), f32),
                        pltpu.VMEM((S5_NGB, rows, LANES), f32),
                        pltpu.VMEM((w, w), bf16),
                        pltpu.VMEM((S5_NGB, SUBLANES, S5_SW), f32)],
        compiler_params=_cparams("arbitrary"),
        name="s5_mixer",
    )(uz, uz, w2, lam2, cm, d.reshape(1, w), glu_w, glu_b.reshape(1, w))


def _gmlp_kernel(u_ref, v_ref, z_ref, lnw_ref, lnb_ref, ws_ref, bias_ref, o_ref,
                 vn_ref, wm_ref):
    rows = u_ref.shape[0]
    v = jax.nn.gelu(v_ref[...].astype(f32))
    mu = jnp.mean(v, axis=-1, keepdims=True)
    vc = v - mu
    var = jnp.mean(vc * vc, axis=-1, keepdims=True)
    vn_ref[...] = (vc * lax.rsqrt(var + EPS) * lnw_ref[...] + lnb_ref[...]).astype(bf16)
    t_idx = lax.broadcasted_iota(jnp.int32, (SG_CHUNK, SG_CHUNK), 0)
    s_idx = lax.broadcasted_iota(jnp.int32, (SG_CHUNK, SG_CHUNK), 1)
    for h in range(SG_HEADS):
        wm_ref[h] = jnp.where(s_idx <= t_idx, ws_ref[h], 0.0).astype(bf16)
    for c in range(rows // SG_CHUNK):
        rs = slice(c * SG_CHUNK, (c + 1) * SG_CHUNK)
        for h in range(SG_HEADS):
            cs = slice(h * LANES, (h + 1) * LANES)
            mixed = jnp.dot(wm_ref[h], vn_ref[rs, cs], preferred_element_type=f32)
            mixed = mixed + bias_ref[:, cs]
            u = jax.nn.gelu(u_ref[rs, cs].astype(f32))
            z = z_ref[rs, cs].astype(f32)
            o_ref[rs, cs] = (u * mixed * _silu(z)).astype(o_ref.dtype)


def _gmlp_mixer(p, ln_w, ln_b, w_s, b_s, rows=512):
    t = p.shape[0]
    w = SG_WIDTH
    bias = jnp.repeat(b_s.T, LANES, axis=1)
    const2 = lambda i: (0, 0)
    return pl.pallas_call(
        _gmlp_kernel,
        grid=(t // rows,),
        in_specs=[pl.BlockSpec((rows, w), lambda i: (i, 0)),
                  pl.BlockSpec((rows, w), lambda i: (i, 1)),
                  pl.BlockSpec((rows, w), lambda i: (i, 2)),
                  pl.BlockSpec((1, w), const2),
                  pl.BlockSpec((1, w), const2),
                  pl.BlockSpec(w_s.shape, lambda i: (0, 0, 0)),
                  pl.BlockSpec((SG_CHUNK, w), const2)],
        out_specs=pl.BlockSpec((rows, w), lambda i: (i, 0)),
        out_shape=jax.ShapeDtypeStruct((t, w), bf16),
        scratch_shapes=[pltpu.VMEM((rows, w), bf16),
                        pltpu.VMEM((SG_HEADS, SG_CHUNK, SG_CHUNK), bf16)],
        compiler_params=_cparams("parallel"),
        name="gmlp_mixer",
    )(p, p, p, ln_w.reshape(1, w), ln_b.reshape(1, w), w_s, bias)


def _rope(x, cos, sin_up, sin_dn):
    half = ROT_DIM // 2
    return (x * cos + pltpu.roll(x, LANES - half, 1) * sin_up
            + pltpu.roll(x, half, 1) * sin_dn)


def _attn_kernel(sink_ref, q_ref, z_ref, kvp_ref, kvc_ref, rp_ref, rc_ref, o_ref):
    n = pl.program_id(1)
    blk = ATT_BLOCK
    cos_c, up_c, dn_c = rc_ref[0], rc_ref[1], rc_ref[2]
    cos_p, up_p, dn_p = rp_ref[0], rp_ref[1], rp_ref[2]

    k_all = jnp.concatenate([
        _rope(kvp_ref[:, :LANES].astype(f32), cos_p, up_p, dn_p),
        _rope(kvc_ref[:, :LANES].astype(f32), cos_c, up_c, dn_c)], axis=0)
    v_all = jnp.concatenate([kvp_ref[:, LANES:].astype(f32),
                             kvc_ref[:, LANES:].astype(f32)], axis=0)
    lane = lax.broadcasted_iota(jnp.int32, (2 * blk, LANES), 1)
    low = lane < HEAD_DIM
    k_sw = pltpu.roll(k_all, HEAD_DIM, 1)
    v_sw = pltpu.roll(v_all, HEAD_DIM, 1)
    k_pad = [[jnp.where(low, k_all, 0.0).astype(bf16), jnp.where(low, 0.0, k_sw).astype(bf16)],
             [jnp.where(low, k_sw, 0.0).astype(bf16), jnp.where(low, 0.0, k_all).astype(bf16)]]
    v_pad = [[jnp.where(low, v_all, 0.0).astype(bf16), jnp.where(low, 0.0, v_sw).astype(bf16)],
             [jnp.where(low, v_sw, 0.0).astype(bf16), jnp.where(low, 0.0, v_all).astype(bf16)]]

    qi = lax.broadcasted_iota(jnp.int32, (blk, 2 * blk), 0)
    ki = lax.broadcasted_iota(jnp.int32, (blk, 2 * blk), 1)
    allowed = (ki > qi) & (ki <= qi + blk) & ((ki >= blk) | (n > 0))

    scale = HEAD_DIM ** -0.5
    for j in range(ATT_HEADS // 2):
        cs = slice(j * LANES, (j + 1) * LANES)
        kh = (2 * j) // (ATT_HEADS // ATT_KV_HEADS)
        q = (_rope(q_ref[:, cs].astype(f32), cos_c, up_c, dn_c) * scale).astype(bf16)
        o = None
        for par in range(2):
            sink = sink_ref[2 * j + par]
            s = lax.dot_general(q, k_pad[kh][par], (((1,), (1,)), ((), ())),
                                preferred_element_type=f32)
            s = jnp.where(allowed, s, NEG_INF)
            m = jnp.maximum(jnp.max(s, axis=-1, keepdims=True), sink)
            p = jnp.exp(s - m)
            den = jnp.sum(p, axis=-1, keepdims=True) + jnp.exp(sink - m)
            pv = jnp.dot(p.astype(bf16), v_pad[kh][par], preferred_element_type=f32)
            pv = pv * (1.0 / den)
            o = pv if o is None else o + pv
        z = z_ref[:, cs].astype(f32)
        o_ref[:, cs] = (o * _silu(z)).astype(o_ref.dtype)


def _rope_tables(seq):
    half = ROT_DIM // 2
    inv_freq = ROPE_THETA ** (-jnp.arange(0, ROT_DIM, 2, dtype=f32) / ROT_DIM)
    ang = jnp.arange(seq, dtype=f32)[:, None] * inv_freq[None, :]
    cos, sin = jnp.cos(ang), jnp.sin(ang)
    ones = jnp.ones((seq, HEAD_DIM - ROT_DIM), f32)
    zeros = jnp.zeros((seq, HEAD_DIM - ROT_DIM), f32)
    zh = jnp.zeros((seq, half), f32)
    cos_t = jnp.concatenate([cos, cos, ones], axis=1)
    up_t = jnp.concatenate([-sin, zh, zeros], axis=1)
    dn_t = jnp.concatenate([zh, sin, zeros], axis=1)
    tab = jnp.stack([cos_t, up_t, dn_t])
    return jnp.concatenate([tab, tab], axis=2)


def _attn_mixer(p, kv, sinks, rope, bsz, seq):
    nb = seq // ATT_BLOCK
    w = ATT_WIDTH
    row = lambda b, n: b * nb + n
    prev = lambda b, n: b * nb + jnp.maximum(n - 1, 0)
    return pl.pallas_call(
        _attn_kernel,
        grid=(bsz, nb),
        in_specs=[pl.BlockSpec(memory_space=pltpu.SMEM),
                  pl.BlockSpec((ATT_BLOCK, w), lambda b, n: (row(b, n), 3)),
                  pl.BlockSpec((ATT_BLOCK, w), lambda b, n: (row(b, n), 4)),
                  pl.BlockSpec((ATT_BLOCK, 2 * KV_WIDTH), lambda b, n: (prev(b, n), 0)),
                  pl.BlockSpec((ATT_BLOCK, 2 * KV_WIDTH), lambda b, n: (row(b, n), 0)),
                  pl.BlockSpec((3, ATT_BLOCK, LANES), lambda b, n: (0, jnp.maximum(n - 1, 0), 0)),
                  pl.BlockSpec((3, ATT_BLOCK, LANES), lambda b, n: (0, n, 0))],
        out_specs=pl.BlockSpec((ATT_BLOCK, w), lambda b, n: (row(b, n), 0)),
        out_shape=jax.ShapeDtypeStruct((bsz * seq, w), bf16),
        compiler_params=_cparams("parallel", "parallel"),
        name="attn_mixer",
    )(sinks, p, p, kv, kv, rope, rope)


def _merge_kernel(ya_ref, yb_ref, yc_ref, g_ref, x_ref, wa_ref, wb_ref, wc_ref, wo_ref,
                  nw_ref, *out_refs, last, chunk):
    merged_ref = out_refs[-1]
    ys = (ya_ref, yb_ref, yc_ref)
    ws = (wa_ref, wb_ref, wc_ref)
    for c in range(D_MODEL // chunk):
        cs = slice(c * chunk, (c + 1) * chunk)
        acc = None
        for k in range(N_BRANCH):
            gate = jax.nn.sigmoid(g_ref[:, k * D_MODEL + c * chunk:k * D_MODEL + (c + 1) * chunk]
                                  .astype(f32))
            term = gate * jnp.dot(ys[k][...], ws[k][:, cs], preferred_element_type=f32)
            acc = term if acc is None else acc + term
        merged_ref[:, cs] = acc.astype(bf16)
    x_new = x_ref[...] + jnp.dot(merged_ref[...], wo_ref[...], preferred_element_type=f32)
    ms = jnp.mean(x_new * x_new, axis=-1, keepdims=True)
    normed = x_new * lax.rsqrt(ms + EPS) * nw_ref[...]
    if last:
        out_refs[0][...] = normed
    else:
        out_refs[0][...] = x_new
        out_refs[1][...] = normed.astype(bf16)


def _merge_out(ya_tm, yb, yc, gates, x, wa, wb, wc, wo, layer, next_norm_w, bsz, seq, last,
               tm=256):
    t = bsz * seq
    nt = seq // tm
    d = D_MODEL
    row = lambda b, n: (b * nt + n, 0)
    const = lambda b, n: (0, 0)
    once = pl.Buffered(1)
    wspec = lambda m: pl.BlockSpec((pl.Squeezed(),) + m.shape[1:], lambda b, n: (layer, 0, 0),
                                   pipeline_mode=once)
    if last:
        out_shape = [jax.ShapeDtypeStruct((t, d), f32)]
        out_specs = [pl.BlockSpec((tm, d), row)]
    else:
        out_shape = [jax.ShapeDtypeStruct((t, d), f32), jax.ShapeDtypeStruct((t, d), bf16)]
        out_specs = [pl.BlockSpec((tm, d), row), pl.BlockSpec((tm, d), row)]
    return pl.pallas_call(
        functools.partial(_merge_kernel, last=last, chunk=512),
        grid=(bsz, nt),
        in_specs=[pl.BlockSpec((tm, SSM_WIDTH), lambda b, n: (n, b)),
                  pl.BlockSpec((tm, SG_WIDTH), row),
                  pl.BlockSpec((tm, ATT_WIDTH), row),
                  pl.BlockSpec((tm, N_BRANCH * d), row),
                  pl.BlockSpec((tm, d), row),
                  wspec(wa), wspec(wb), wspec(wc), wspec(wo),
                  pl.BlockSpec((1, d), const)],
        out_specs=out_specs,
        out_shape=out_shape,
        scratch_shapes=[pltpu.VMEM((tm, d), bf16)],
        compiler_params=_cparams("parallel", "parallel"),
        name="merge_out",
    )(ya_tm, yb, yc, gates, x, wa, wb, wc, wo, next_norm_w.reshape(1, d))


def kernel(x, norm_w, w_in, ssm_a_re, ssm_a_im, ssm_log_dt, ssm_b_re, ssm_b_im, ssm_c_re, ssm_c_im, ssm_d, ssm_glu_w, ssm_glu_b, sg_ln_w, sg_ln_b, sg_w, sg_b, attn_sinks, w_branch_a, w_branch_b, w_branch_c, w_out, final_norm_w):
    bsz, seq, d = x.shape
    assert 2 * bsz == SUBLANES, "the S5 kernel packs two timesteps of all batches per sublane tile"
    depth = norm_w.shape[0]
    t = bsz * seq
    xf = x.reshape(t, d)
    rope = _rope_tables(seq)

    o_ub = 2 * SSM_WIDTH
    o_q = o_ub + 3 * SG_WIDTH
    o_k = o_q + ATT_WIDTH
    o_zc = o_k + 2 * KV_WIDTH
    o_g = o_zc + ATT_WIDTH

    tn = 1024
    s5_cols = [0, SSM_WIDTH]
    bc_cols = [o_ub, o_ub + tn, o_ub + 2 * tn, o_q, o_zc]
    gate_cols = [o_g + c * tn for c in range(N_BRANCH * d // tn)]

    w2s, lam2s, cms = jax.vmap(_s5_params)(ssm_a_re, ssm_a_im, ssm_log_dt, ssm_b_re,
                                           ssm_b_im, ssm_c_re, ssm_c_im)
    wa_bf, wb_bf, wc_bf, wo_bf = (m.astype(bf16) for m in
                                  (w_branch_a, w_branch_b, w_branch_c, w_out))

    h = _rmsnorm(xf, norm_w[0], bf16)
    out = None
    for l in range(depth):
        uz = _project(h, w_in, l, s5_cols, tn, time_major_batches=bsz, name="proj_s5")
        p_bc = _project(h, w_in, l, bc_cols, tn, name="proj_bc")
        kv = _project(h, w_in, l, [o_k], 2 * KV_WIDTH, name="proj_kv")
        gates = _project(h, w_in, l, gate_cols, tn, name="proj_gates")

        ya = _s5_mixer(uz, w2s[l], lam2s[l], cms[l], ssm_d[l], ssm_glu_w, l,
                       ssm_glu_b[l], bsz)
        yb = _gmlp_mixer(p_bc, sg_ln_w[l], sg_ln_b[l], sg_w[l], sg_b[l])
        yc = _attn_mixer(p_bc, kv, attn_sinks[l], rope, bsz, seq)

        last = l == depth - 1
        next_w = final_norm_w if last else norm_w[l + 1]
        res = _merge_out(ya, yb, yc, gates, xf, wa_bf, wb_bf, wc_bf, wo_bf, l,
                         next_w, bsz, seq, last)
        if last:
            out = res[0]
        else:
            xf, h = res
    return out.reshape(bsz, seq, d)
```

```python
import functools
import math
from typing import Callable, NamedTuple

import jax
import jax.numpy as jnp
from jax import lax
from jax.experimental import pallas as pl
from jax.experimental.pallas import tpu as pltpu

f32 = jnp.float32
bf16 = jnp.bfloat16

D_MODEL = 2048
EPS = 1e-6
NEG_INF = -1e30

SSM_WIDTH = D_MODEL // 2
SSM_GROUP = 16
SSM_GROUPS = SSM_WIDTH // SSM_GROUP
SSM_STATE = 64

SG_WIDTH = D_MODEL // 2
SG_HEADS = 8
SG_CHUNK = 128

HEAD_DIM = 64
ATT_HEADS = D_MODEL // 128
ATT_KV_HEADS = ATT_HEADS // 8
ATT_WIDTH = ATT_HEADS * HEAD_DIM
KV_WIDTH = ATT_KV_HEADS * HEAD_DIM
ATT_BLOCK = 128
ROT_DIM = HEAD_DIM // 4
ROPE_THETA = 500000.0
N_BRANCH = 3

LANES = 128
SUBLANES = 8
MXU_DIM = 256
VMEM_LIMIT_BYTES = 48 * 1024 * 1024

S5_GB = LANES // SSM_GROUP
S5_NGB = SSM_GROUPS // S5_GB
S5_HALF = S5_GB * SSM_STATE
S5_SW = 2 * S5_HALF


def _cparams(*sem):
    return pltpu.CompilerParams(dimension_semantics=sem, vmem_limit_bytes=VMEM_LIMIT_BYTES)


def _silu(x):
    return x * jax.nn.sigmoid(x)


def _rmsnorm_kernel(x_ref, w_ref, o_ref):
    x = x_ref[...]
    ms = jnp.mean(x * x, axis=-1, keepdims=True)
    o_ref[...] = (x * lax.rsqrt(ms + EPS) * w_ref[...]).astype(o_ref.dtype)


def _rmsnorm(x, w, out_dtype, tm=512):
    t, d = x.shape
    return pl.pallas_call(
        _rmsnorm_kernel,
        grid=(t // tm,),
        in_specs=[pl.BlockSpec((tm, d), lambda i: (i, 0)),
                  pl.BlockSpec((1, d), lambda i: (0, 0))],
        out_specs=pl.BlockSpec((tm, d), lambda i: (i, 0)),
        out_shape=jax.ShapeDtypeStruct((t, d), out_dtype),
        compiler_params=_cparams("parallel"),
        name="rmsnorm",
    )(x, w.reshape(1, d))


class _SideJob(NamedTuple):
    steps: int
    body: Callable
    inputs: tuple
    in_specs: tuple
    out_shapes: tuple
    out_specs: tuple
    scratch: tuple


def _proj_kernel(offs_ref, h_ref, w_ref, *refs, side_body, n_side_in, n_side_out):
    del offs_ref
    side_in = refs[:n_side_in]
    o_ref = refs[n_side_in]
    side_out = refs[n_side_in + 1:n_side_in + 1 + n_side_out]
    wbf_ref = refs[n_side_in + 1 + n_side_out]
    side_scratch = refs[n_side_in + 2 + n_side_out:]

    @pl.when(pl.program_id(1) == 0)
    def _():
        wbf_ref[...] = w_ref[...].astype(bf16)

    parts = []
    if side_body is not None:
        step = pl.program_id(0) * pl.num_programs(1) + pl.program_id(1)
        parts = side_body(step, *side_in, *side_out, *side_scratch)
    tm, tn = o_ref.shape
    if not parts:
        o_ref[...] = jnp.dot(h_ref[...], wbf_ref[...],
                             preferred_element_type=f32).astype(o_ref.dtype)
        return
    pieces = [(rs, cs) for cs in range(0, tn, MXU_DIM) for rs in range(0, tm, MXU_DIM)]
    n_parts, n_pieces = len(parts), len(pieces)
    done = 0
    for k, (rs, cs) in enumerate(pieces):
        upto = -(-(k + 1) * n_parts // n_pieces)
        for part in parts[done:upto]:
            part()
        done = upto
        o_ref[rs:rs + MXU_DIM, cs:cs + MXU_DIM] = jnp.dot(
            h_ref[rs:rs + MXU_DIM, :], wbf_ref[:, cs:cs + MXU_DIM],
            preferred_element_type=f32).astype(o_ref.dtype)


def _project(h, w_all, layer, col_offsets, tn, tm=1024, time_major_batches=None, side=None,
             name="proj"):
    t, k = h.shape
    nj = len(col_offsets)
    ni = t // tm
    assert all(off % LANES == 0 for off in col_offsets)
    assert side is None or side.steps == nj * ni
    offs = jnp.asarray([off // LANES for off in col_offsets], jnp.int32)
    if time_major_batches is None:
        out_shape = (t, nj * tn)
        out_map = lambda j, i, offs_ref: (i, j)
    else:
        seq = t // time_major_batches
        nt = seq // tm
        out_shape = (seq, nj * time_major_batches * tn)
        out_map = lambda j, i, offs_ref: (i % nt, j * time_major_batches + i // nt)

    def step_spec(spec):
        if spec is None:
            return pl.BlockSpec(memory_space=pltpu.SMEM)
        shape, index = spec
        return pl.BlockSpec(shape, lambda j, i, offs_ref: index(j * ni + i))

    side_in_specs = [] if side is None else [step_spec(sp) for sp in side.in_specs]
    side_out_specs = [] if side is None else [step_spec(sp) for sp in side.out_specs]
    side_inputs = () if side is None else side.inputs
    side_out_shapes = [] if side is None else list(side.out_shapes)
    side_scratch = [] if side is None else list(side.scratch)
    grid_spec = pltpu.PrefetchScalarGridSpec(
        num_scalar_prefetch=1,
        grid=(nj, ni),
        in_specs=[pl.BlockSpec((tm, k), lambda j, i, offs_ref: (i, 0)),
                  pl.BlockSpec((pl.Squeezed(), pl.Element(k), pl.Element(tn)),
                               lambda j, i, offs_ref: (layer, 0, offs_ref[j] * LANES))]
                 + side_in_specs,
        out_specs=[pl.BlockSpec((tm, tn), out_map)] + side_out_specs,
        scratch_shapes=[pltpu.VMEM((k, tn), bf16)] + side_scratch)
    res = pl.pallas_call(
        functools.partial(_proj_kernel, side_body=None if side is None else side.body,
                          n_side_in=len(side_in_specs), n_side_out=len(side_out_specs)),
        grid_spec=grid_spec,
        out_shape=[jax.ShapeDtypeStruct(out_shape, bf16)] + side_out_shapes,
        compiler_params=_cparams("parallel", "arbitrary"),
        name=name,
    )(offs, h, w_all, *side_inputs)
    return res[0] if side is None else res


def _s5_kernel(u_ref, z_ref, w2_ref, lam2_ref, cm_ref, d_ref, gw_ref, gb_ref, o_ref,
               ubuf, bu2_ref, st2_ref, y_ref, gwbf_ref, carry_ref):
    i = pl.program_id(0)
    steps = u_ref.shape[0]
    bsz = SUBLANES // 2
    rows = steps * bsz
    w = SSM_WIDTH

    @pl.when(i == 0)
    def _():
        carry_ref[...] = jnp.zeros_like(carry_ref)
        ubuf[:, 0:SUBLANES, :] = jnp.zeros((S5_NGB, SUBLANES, LANES), f32)
        gwbf_ref[...] = gw_ref[...].astype(bf16)

    @pl.when(i > 0)
    def _():
        ubuf[:, 0:SUBLANES, :] = ubuf[:, rows:rows + SUBLANES, :]

    for g in range(S5_NGB):
        for b in range(bsz):
            ubuf[g, pl.ds(SUBLANES + b, steps, stride=bsz), :] = (
                u_ref[:, b * w + g * LANES:b * w + (g + 1) * LANES].astype(f32))

    for g in range(S5_NGB):
        cs = slice(g * LANES, (g + 1) * LANES)
        lhs = jnp.concatenate(
            [ubuf[g, SUBLANES:rows + SUBLANES, :].astype(bf16),
             ubuf[g, SUBLANES - bsz:rows + SUBLANES - bsz, :].astype(bf16)], axis=1)
        bu_ref = bu2_ref.at[g % 2]
        st_ref = st2_ref.at[g % 2]
        bu_ref[...] = jnp.dot(lhs, w2_ref[g], preferred_element_type=f32)
        l_re = lam2_ref[g, :, :S5_HALF]
        l_im = lam2_ref[g, :, S5_HALF:]

        def body(r, s, l_re=l_re, l_im=l_im):
            s_re, s_im = s
            off = pl.multiple_of(r * SUBLANES, SUBLANES)
            b = bu_ref[pl.ds(off, SUBLANES), :]
            n_re = l_re * s_re - l_im * s_im + b[:, :S5_HALF]
            n_im = l_re * s_im + l_im * s_re + b[:, S5_HALF:]
            st_ref[pl.ds(off, SUBLANES), :S5_HALF] = n_re
            st_ref[pl.ds(off, SUBLANES), S5_HALF:] = n_im
            return n_re, n_im

        s0 = (carry_ref[g, :, :S5_HALF], carry_ref[g, :, S5_HALF:])
        s_re, s_im = lax.fori_loop(0, rows // SUBLANES, body, s0, unroll=True)
        carry_ref[g, :, :S5_HALF] = s_re
        carry_ref[g, :, S5_HALF:] = s_im
        y_g = jnp.dot(st_ref[...].astype(bf16), cm_ref[g], preferred_element_type=f32)
        y_ref[g] = y_g + d_ref[:, cs] * ubuf[g, SUBLANES:rows + SUBLANES, :]

    y = jax.nn.gelu(jnp.concatenate([y_ref[g] for g in range(S5_NGB)], axis=1))
    gate = jnp.dot(y.astype(bf16), gwbf_ref[...], preferred_element_type=f32) + gb_ref[...]
    y = y * jax.nn.sigmoid(gate)
    for g in range(S5_NGB):
        y_ref[g] = y[:, g * LANES:(g + 1) * LANES]
    for b in range(bsz):
        for g in range(S5_NGB):
            cs = slice(b * w + g * LANES, b * w + (g + 1) * LANES)
            z = z_ref[:, cs].astype(f32)
            o_ref[:, cs] = (y_ref[g, pl.ds(b, steps, stride=bsz), :]
                            * _silu(z)).astype(o_ref.dtype)


def _s5_params(a_re, a_im, log_dt, b_re, b_im, c_re, c_im):
    dt = jnp.exp(log_dt)[:, None]
    mag = jnp.exp(a_re * dt)
    lb_re = mag * jnp.cos(a_im * dt)
    lb_im = mag * jnp.sin(a_im * dt)
    den = a_re * a_re + a_im * a_im
    k_re = ((lb_re - 1.0) * a_re + lb_im * a_im) / den
    k_im = (lb_im * a_re - (lb_re - 1.0) * a_im) / den
    bb_re = k_re[..., None] * b_re - k_im[..., None] * b_im
    bb_im = k_re[..., None] * b_im + k_im[..., None] * b_re
    t_re = lb_re[..., None] * bb_re - lb_im[..., None] * bb_im
    t_im = lb_re[..., None] * bb_im + lb_im[..., None] * bb_re
    eye = jnp.eye(S5_GB, dtype=f32)

    def in_block(m):
        m = m.reshape(S5_NGB, S5_GB, SSM_STATE, SSM_GROUP)
        return jnp.einsum('agpc,gh->agchp', m, eye).reshape(S5_NGB, LANES, S5_HALF)

    w2 = jnp.concatenate([
        jnp.concatenate([in_block(bb_re), in_block(bb_im)], axis=2),
        jnp.concatenate([in_block(t_re), in_block(t_im)], axis=2)], axis=1)

    def out_block(m):
        m = m.reshape(S5_NGB, S5_GB, SSM_GROUP, SSM_STATE)
        return jnp.einsum('agcp,gh->agphc', m, eye).reshape(S5_NGB, S5_HALF, LANES)

    cm = jnp.concatenate([out_block(c_re), out_block(-c_im)], axis=1)
    l2_re = (lb_re * lb_re - lb_im * lb_im).reshape(S5_NGB, 1, S5_HALF)
    l2_im = (2.0 * lb_re * lb_im).reshape(S5_NGB, 1, S5_HALF)
    lam2 = jnp.broadcast_to(jnp.concatenate([l2_re, l2_im], axis=2),
                            (S5_NGB, SUBLANES, S5_SW))
    return w2.astype(bf16), lam2, cm.astype(bf16)


def _s5_mixer(uz, w2, lam2, cm, d, glu_w, layer, glu_b, bsz, steps=128):
    seq = uz.shape[0]
    w = SSM_WIDTH
    bw = bsz * w
    rows = steps * bsz
    const3 = lambda i: (0, 0, 0)
    const2 = lambda i: (0, 0)
    return pl.pallas_call(
        _s5_kernel,
        grid=(seq // steps,),
        in_specs=[pl.BlockSpec((steps, bw), lambda i: (i, 0)),
                  pl.BlockSpec((steps, bw), lambda i: (i, 1)),
                  pl.BlockSpec(w2.shape, const3),
                  pl.BlockSpec(lam2.shape, const3),
                  pl.BlockSpec(cm.shape, const3),
                  pl.BlockSpec((1, w), const2),
                  pl.BlockSpec((pl.Squeezed(), w, w), lambda i: (layer, 0, 0)),
                  pl.BlockSpec((1, w), const2)],
        out_specs=pl.BlockSpec((steps, bw), lambda i: (i, 0)),
        out_shape=jax.ShapeDtypeStruct((seq, bw), bf16),
        scratch_shapes=[pltpu.VMEM((S5_NGB, rows + SUBLANES, LANES), f32),
                        pltpu.VMEM((2, rows, S5_SW), f32),
                        pltpu.VMEM((2, rows, S5_SW), f32),
                        pltpu.VMEM((S5_NGB, rows, LANES), f32),
                        pltpu.VMEM((w, w), bf16),
                        pltpu.VMEM((S5_NGB, SUBLANES, S5_SW), f32)],
        compiler_params=_cparams("arbitrary"),
        name="s5_mixer",
    )(uz, uz, w2, lam2, cm, d.reshape(1, w), glu_w, glu_b.reshape(1, w))


def _gmlp_body(step, u_ref, v_ref, z_ref, lnw_ref, lnb_ref, ws_ref, bias_ref, o_ref,
               vn_ref, wm_ref):
    del step
    rows = u_ref.shape[0]

    def norm_part(c):
        rs = slice(c * SG_CHUNK, (c + 1) * SG_CHUNK)
        if c == 0:
            t_idx = lax.broadcasted_iota(jnp.int32, (SG_CHUNK, SG_CHUNK), 0)
            s_idx = lax.broadcasted_iota(jnp.int32, (SG_CHUNK, SG_CHUNK), 1)
            for h in range(SG_HEADS):
                wm_ref[h] = jnp.where(s_idx <= t_idx, ws_ref[h], 0.0).astype(bf16)
        v = jax.nn.gelu(v_ref[rs, :].astype(f32))
        mu = jnp.mean(v, axis=-1, keepdims=True)
        vc = v - mu
        var = jnp.mean(vc * vc, axis=-1, keepdims=True)
        vn_ref[rs, :] = (vc * lax.rsqrt(var + EPS) * lnw_ref[...] + lnb_ref[...]).astype(bf16)

    def mix_part(c, h):
        rs = slice(c * SG_CHUNK, (c + 1) * SG_CHUNK)
        cs = slice(h * LANES, (h + 1) * LANES)
        mixed = jnp.dot(wm_ref[h], vn_ref[rs, cs], preferred_element_type=f32)
        mixed = mixed + bias_ref[:, cs]
        u = jax.nn.gelu(u_ref[rs, cs].astype(f32))
        z = z_ref[rs, cs].astype(f32)
        o_ref[rs, cs] = (u * mixed * _silu(z)).astype(o_ref.dtype)

    parts = []
    for c in range(rows // SG_CHUNK):
        parts.append(functools.partial(norm_part, c))
        parts.extend(functools.partial(mix_part, c, h) for h in range(SG_HEADS))
    return parts


def _gmlp_job(p, ln_w, ln_b, w_s, b_s, rows=512):
    t = p.shape[0]
    w = SG_WIDTH
    bias = jnp.repeat(b_s.T, LANES, axis=1)
    const2 = lambda step: (0, 0)
    return _SideJob(
        steps=t // rows,
        body=_gmlp_body,
        inputs=(p, p, p, ln_w.reshape(1, w), ln_b.reshape(1, w), w_s, bias),
        in_specs=(((rows, w), lambda step: (step, 0)),
                  ((rows, w), lambda step: (step, 1)),
                  ((rows, w), lambda step: (step, 2)),
                  ((1, w), const2),
                  ((1, w), const2),
                  (w_s.shape, lambda step: (0, 0, 0)),
                  ((SG_CHUNK, w), const2)),
        out_shapes=(jax.ShapeDtypeStruct((t, w), bf16),),
        out_specs=(((rows, w), lambda step: (step, 0)),),
        scratch=(pltpu.VMEM((rows, w), bf16),
                 pltpu.VMEM((SG_HEADS, SG_CHUNK, SG_CHUNK), bf16)))


def _rope(x, cos, sin_up, sin_dn):
    half = ROT_DIM // 2
    return (x * cos + pltpu.roll(x, LANES - half, 1) * sin_up
            + pltpu.roll(x, half, 1) * sin_dn)


def _attn_body(nb, step, sink_ref, q_ref, z_ref, kvp_ref, kvc_ref, rp_ref, rc_ref, o_ref):
    n = step % nb
    blk = ATT_BLOCK
    cos_c, up_c, dn_c = rc_ref[0], rc_ref[1], rc_ref[2]
    cos_p, up_p, dn_p = rp_ref[0], rp_ref[1], rp_ref[2]

    k_all = jnp.concatenate([
        _rope(kvp_ref[:, :LANES].astype(f32), cos_p, up_p, dn_p),
        _rope(kvc_ref[:, :LANES].astype(f32), cos_c, up_c, dn_c)], axis=0)
    v_all = jnp.concatenate([kvp_ref[:, LANES:].astype(f32),
                             kvc_ref[:, LANES:].astype(f32)], axis=0)
    lane = lax.broadcasted_iota(jnp.int32, (2 * blk, LANES), 1)
    low = lane < HEAD_DIM
    k_sw = pltpu.roll(k_all, HEAD_DIM, 1)
    v_sw = pltpu.roll(v_all, HEAD_DIM, 1)
    k_pad = [[jnp.where(low, k_all, 0.0).astype(bf16), jnp.where(low, 0.0, k_sw).astype(bf16)],
             [jnp.where(low, k_sw, 0.0).astype(bf16), jnp.where(low, 0.0, k_all).astype(bf16)]]
    v_pad = [[jnp.where(low, v_all, 0.0).astype(bf16), jnp.where(low, 0.0, v_sw).astype(bf16)],
             [jnp.where(low, v_sw, 0.0).astype(bf16), jnp.where(low, 0.0, v_all).astype(bf16)]]

    qi = lax.broadcasted_iota(jnp.int32, (blk, blk), 0)
    kj = lax.broadcasted_iota(jnp.int32, (blk, blk), 1)
    in_cur = kj <= qi
    prev_bias = jnp.where(in_cur | (n > 0), 0.0, NEG_INF)

    scale = HEAD_DIM ** -0.5
    cos_q, up_q, dn_q = cos_c * scale, up_c * scale, dn_c * scale
    group = ATT_HEADS // ATT_KV_HEADS

    def staged(tile_lo, tile_hi):
        heads = range(2 * tile_lo, 2 * tile_hi)
        tiles = {j: slice(j * LANES, (j + 1) * LANES) for j in range(tile_lo, tile_hi)}
        sinks = {hd: sink_ref[hd] for hd in heads}
        st = {}

        def scores():
            qs = {j: _rope(q_ref[:, cs].astype(f32), cos_q, up_q, dn_q).astype(bf16)
                  for j, cs in tiles.items()}
            s2 = {hd: lax.dot_general(qs[hd // 2], k_pad[hd // group][hd % 2],
                                      (((1,), (1,)), ((), ())), preferred_element_type=f32)
                  for hd in heads}
            st["s"] = {hd: jnp.where(in_cur, s2[hd][:, blk:], s2[hd][:, :blk]) + prev_bias
                       for hd in heads}

        def row_max():
            st["m"] = {hd: jnp.maximum(jnp.max(st["s"][hd], axis=-1, keepdims=True), sinks[hd])
                       for hd in heads}

        def exponent():
            st["p"] = {hd: jnp.exp(st["s"][hd] - st["m"][hd]) for hd in heads}
            st["den"] = {hd: jnp.sum(st["p"][hd], axis=-1, keepdims=True)
                         + jnp.exp(sinks[hd] - st["m"][hd]) for hd in heads}

        def values():
            p = st["p"]
            p2 = {hd: jnp.concatenate([jnp.where(in_cur, 0.0, p[hd]),
                                       jnp.where(in_cur, p[hd], 0.0)], axis=1).astype(bf16)
                  for hd in heads}
            st["pv"] = {hd: jnp.dot(p2[hd], v_pad[hd // group][hd % 2],
                                    preferred_element_type=f32) * (1.0 / st["den"][hd])
                        for hd in heads}

        def output():
            pv = st["pv"]
            for j, cs in tiles.items():
                z = z_ref[:, cs].astype(f32)
                o_ref[:, cs] = ((pv[2 * j] + pv[2 * j + 1]) * _silu(z)).astype(o_ref.dtype)

        return [scores, row_max, exponent, values, output]

    n_tiles = ATT_HEADS // 2
    bounds = [0, 3, 6, n_tiles]
    return [stage for lo, hi in zip(bounds[:-1], bounds[1:]) for stage in staged(lo, hi)]


def _rope_tables(seq):
    half = ROT_DIM // 2
    inv_freq = ROPE_THETA ** (-jnp.arange(0, ROT_DIM, 2, dtype=f32) / ROT_DIM)
    ang = jnp.arange(seq, dtype=f32)[:, None] * inv_freq[None, :]
    cos, sin = jnp.cos(ang), jnp.sin(ang)
    ones = jnp.ones((seq, HEAD_DIM - ROT_DIM), f32)
    zeros = jnp.zeros((seq, HEAD_DIM - ROT_DIM), f32)
    zh = jnp.zeros((seq, half), f32)
    cos_t = jnp.concatenate([cos, cos, ones], axis=1)
    up_t = jnp.concatenate([-sin, zh, zeros], axis=1)
    dn_t = jnp.concatenate([zh, sin, zeros], axis=1)
    tab = jnp.stack([cos_t, up_t, dn_t])
    return jnp.concatenate([tab, tab], axis=2)


def _attn_job(p, kv, sinks, rope, bsz, seq):
    nb = seq // ATT_BLOCK
    w = ATT_WIDTH
    prev = lambda step: step - jnp.minimum(step % nb, 1)
    return _SideJob(
        steps=bsz * nb,
        body=functools.partial(_attn_body, nb),
        inputs=(sinks, p, p, kv, kv, rope, rope),
        in_specs=(None,
                  ((ATT_BLOCK, w), lambda step: (step, 3)),
                  ((ATT_BLOCK, w), lambda step: (step, 4)),
                  ((ATT_BLOCK, 2 * KV_WIDTH), lambda step: (prev(step), 0)),
                  ((ATT_BLOCK, 2 * KV_WIDTH), lambda step: (step, 0)),
                  ((3, ATT_BLOCK, LANES), lambda step: (0, prev(step) % nb, 0)),
                  ((3, ATT_BLOCK, LANES), lambda step: (0, step % nb, 0))),
        out_shapes=(jax.ShapeDtypeStruct((bsz * seq, w), bf16),),
        out_specs=(((ATT_BLOCK, w), lambda step: (step, 0)),),
        scratch=())


def _merge_kernel(ya_ref, yb_ref, yc_ref, g_ref, x_ref, wa_ref, wb_ref, wc_ref, wo_ref,
                  nw_ref, *out_refs, last, chunk):
    merged_ref = out_refs[-1]
    ys = (ya_ref, yb_ref, yc_ref)
    ws = (wa_ref, wb_ref, wc_ref)
    for c in range(D_MODEL // chunk):
        cs = slice(c * chunk, (c + 1) * chunk)
        acc = None
        for k in range(N_BRANCH):
            gate = jax.nn.sigmoid(g_ref[:, k * D_MODEL + c * chunk:k * D_MODEL + (c + 1) * chunk]
                                  .astype(f32))
            term = gate * jnp.dot(ys[k][...], ws[k][:, cs], preferred_element_type=f32)
            acc = term if acc is None else acc + term
        merged_ref[:, cs] = acc.astype(bf16)
    x_new = x_ref[...] + jnp.dot(merged_ref[...], wo_ref[...], preferred_element_type=f32)
    ms = jnp.mean(x_new * x_new, axis=-1, keepdims=True)
    normed = x_new * lax.rsqrt(ms + EPS) * nw_ref[...]
    if last:
        out_refs[0][...] = normed
    else:
        out_refs[0][...] = x_new
        out_refs[1][...] = normed.astype(bf16)


def _merge_out(ya_tm, yb, yc, gates, x, wa, wb, wc, wo, layer, next_norm_w, bsz, seq, last,
               tm=256):
    t = bsz * seq
    nt = seq // tm
    d = D_MODEL
    row = lambda b, n: (b * nt + n, 0)
    const = lambda b, n: (0, 0)
    once = pl.Buffered(1)
    wspec = lambda m: pl.BlockSpec((pl.Squeezed(),) + m.shape[1:], lambda b, n: (layer, 0, 0),
                                   pipeline_mode=once)
    if last:
        out_shape = [jax.ShapeDtypeStruct((t, d), f32)]
        out_specs = [pl.BlockSpec((tm, d), row)]
    else:
        out_shape = [jax.ShapeDtypeStruct((t, d), f32), jax.ShapeDtypeStruct((t, d), bf16)]
        out_specs = [pl.BlockSpec((tm, d), row), pl.BlockSpec((tm, d), row)]
    return pl.pallas_call(
        functools.partial(_merge_kernel, last=last, chunk=512),
        grid=(bsz, nt),
        in_specs=[pl.BlockSpec((tm, SSM_WIDTH), lambda b, n: (n, b)),
                  pl.BlockSpec((tm, SG_WIDTH), row),
                  pl.BlockSpec((tm, ATT_WIDTH), row),
                  pl.BlockSpec((tm, N_BRANCH * d), row),
                  pl.BlockSpec((tm, d), row),
                  wspec(wa), wspec(wb), wspec(wc), wspec(wo),
                  pl.BlockSpec((1, d), const)],
        out_specs=out_specs,
        out_shape=out_shape,
        scratch_shapes=[pltpu.VMEM((tm, d), bf16)],
        compiler_params=_cparams("parallel", "parallel"),
        name="merge_out",
    )(ya_tm, yb, yc, gates, x, wa, wb, wc, wo, next_norm_w.reshape(1, d))


def kernel(x, norm_w, w_in, ssm_a_re, ssm_a_im, ssm_log_dt, ssm_b_re, ssm_b_im, ssm_c_re, ssm_c_im, ssm_d, ssm_glu_w, ssm_glu_b, sg_ln_w, sg_ln_b, sg_w, sg_b, attn_sinks, w_branch_a, w_branch_b, w_branch_c, w_out, final_norm_w):
    bsz, seq, d = x.shape
    assert 2 * bsz == SUBLANES, "the S5 kernel packs two timesteps of all batches per sublane tile"
    depth = norm_w.shape[0]
    t = bsz * seq
    xf = x.reshape(t, d)
    rope = _rope_tables(seq)

    o_ub = 2 * SSM_WIDTH
    o_q = o_ub + 3 * SG_WIDTH
    o_k = o_q + ATT_WIDTH
    o_zc = o_k + 2 * KV_WIDTH
    o_g = o_zc + ATT_WIDTH

    tn = 1024
    gate_tn = 768
    s5_cols = [0, SSM_WIDTH]
    bc_cols = [o_ub, o_ub + tn, o_ub + 2 * tn, o_q, o_zc]
    gate_cols = [o_g + c * gate_tn for c in range(N_BRANCH * d // gate_tn)]

    w2s, lam2s, cms = jax.vmap(_s5_params)(ssm_a_re, ssm_a_im, ssm_log_dt, ssm_b_re,
                                           ssm_b_im, ssm_c_re, ssm_c_im)
    wa_bf, wb_bf, wc_bf, wo_bf = (m.astype(bf16) for m in
                                  (w_branch_a, w_branch_b, w_branch_c, w_out))

    h = _rmsnorm(xf, norm_w[0], bf16)
    out = None
    for l in range(depth):
        p_bc = _project(h, w_in, l, bc_cols, tn, name="proj_bc")
        kv = _project(h, w_in, l, [o_k], 2 * KV_WIDTH, name="proj_kv")
        uz, yb = _project(h, w_in, l, s5_cols, tn, time_major_batches=bsz,
                          side=_gmlp_job(p_bc, sg_ln_w[l], sg_ln_b[l], sg_w[l], sg_b[l]),
                          name="proj_s5_gmlp")
        gates, yc = _project(h, w_in, l, gate_cols, gate_tn,
                             side=_attn_job(p_bc, kv, attn_sinks[l], rope, bsz, seq),
                             name="proj_gates_attn")
        ya = _s5_mixer(uz, w2s[l], lam2s[l], cms[l], ssm_d[l], ssm_glu_w, l,
                       ssm_glu_b[l], bsz)

        last = l == depth - 1
        next_w = final_norm_w if last else norm_w[l + 1]
        res = _merge_out(ya, yb, yc, gates, xf, wa_bf, wb_bf, wc_bf, wo_bf, l,
                         next_w, bsz, seq, last)
        if last:
            out = res[0]
        else:
            xf, h = res
    return out.reshape(bsz, seq, d)
```

```python
import functools
import math
from typing import Callable, NamedTuple

import jax
import jax.numpy as jnp
from jax import lax
from jax.experimental import pallas as pl
from jax.experimental.pallas import tpu as pltpu

f32 = jnp.float32
bf16 = jnp.bfloat16

D_MODEL = 2048
EPS = 1e-6
NEG_INF = -1e30

SSM_WIDTH = D_MODEL // 2
SSM_GROUP = 16
SSM_GROUPS = SSM_WIDTH // SSM_GROUP
SSM_STATE = 64

SG_WIDTH = D_MODEL // 2
SG_HEADS = 8
SG_CHUNK = 128

HEAD_DIM = 64
ATT_HEADS = D_MODEL // 128
ATT_KV_HEADS = ATT_HEADS // 8
ATT_WIDTH = ATT_HEADS * HEAD_DIM
KV_WIDTH = ATT_KV_HEADS * HEAD_DIM
ATT_BLOCK = 128
ROT_DIM = HEAD_DIM // 4
ROPE_THETA = 500000.0
N_BRANCH = 3

LANES = 128
SUBLANES = 8
MXU_DIM = 256
VMEM_LIMIT_BYTES = 48 * 1024 * 1024

S5_GB = LANES // SSM_GROUP
S5_NGB = SSM_GROUPS // S5_GB
S5_HALF = S5_GB * SSM_STATE
S5_SW = 2 * S5_HALF


def _cparams(*sem):
    return pltpu.CompilerParams(dimension_semantics=sem, vmem_limit_bytes=VMEM_LIMIT_BYTES)


def _silu(x):
    return x * jax.nn.sigmoid(x)


def _rmsnorm_kernel(x_ref, w_ref, o_ref):
    x = x_ref[...]
    ms = jnp.mean(x * x, axis=-1, keepdims=True)
    o_ref[...] = (x * lax.rsqrt(ms + EPS) * w_ref[...]).astype(o_ref.dtype)


def _rmsnorm(x, w, out_dtype, tm=512):
    t, d = x.shape
    return pl.pallas_call(
        _rmsnorm_kernel,
        grid=(t // tm,),
        in_specs=[pl.BlockSpec((tm, d), lambda i: (i, 0)),
                  pl.BlockSpec((1, d), lambda i: (0, 0))],
        out_specs=pl.BlockSpec((tm, d), lambda i: (i, 0)),
        out_shape=jax.ShapeDtypeStruct((t, d), out_dtype),
        compiler_params=_cparams("parallel"),
        name="rmsnorm",
    )(x, w.reshape(1, d))


class _SideJob(NamedTuple):
    steps: int
    body: Callable
    inputs: tuple
    in_specs: tuple
    out_shapes: tuple
    out_specs: tuple
    scratch: tuple


def _proj_kernel(offs_ref, h_ref, w_ref, *refs, side_body, n_side_in, n_side_out):
    del offs_ref
    side_in = refs[:n_side_in]
    o_ref = refs[n_side_in]
    side_out = refs[n_side_in + 1:n_side_in + 1 + n_side_out]
    wbf_ref = refs[n_side_in + 1 + n_side_out]
    side_scratch = refs[n_side_in + 2 + n_side_out:]

    @pl.when(pl.program_id(1) == 0)
    def _():
        wbf_ref[...] = w_ref[...].astype(bf16)

    parts = []
    if side_body is not None:
        step = pl.program_id(0) * pl.num_programs(1) + pl.program_id(1)
        parts = side_body(step, *side_in, *side_out, *side_scratch)
    tm, tn = o_ref.shape
    if not parts:
        o_ref[...] = jnp.dot(h_ref[...], wbf_ref[...],
                             preferred_element_type=f32).astype(o_ref.dtype)
        return
    pieces = [(rs, cs) for cs in range(0, tn, MXU_DIM) for rs in range(0, tm, MXU_DIM)]
    n_parts, n_gaps = len(parts), len(pieces) - 1
    done = 0
    for k, (rs, cs) in enumerate(pieces):
        o_ref[rs:rs + MXU_DIM, cs:cs + MXU_DIM] = jnp.dot(
            h_ref[rs:rs + MXU_DIM, :], wbf_ref[:, cs:cs + MXU_DIM],
            preferred_element_type=f32).astype(o_ref.dtype)
        upto = min(n_parts, -(-(k + 1) * n_parts // n_gaps))
        for part in parts[done:upto]:
            part()
        done = upto


def _project(h, w_all, layer, col_offsets, tn, tm=1024, time_major_batches=None, side=None,
             name="proj"):
    t, k = h.shape
    nj = len(col_offsets)
    ni = t // tm
    assert all(off % LANES == 0 for off in col_offsets)
    assert side is None or side.steps == nj * ni
    offs = jnp.asarray([off // LANES for off in col_offsets], jnp.int32)
    if time_major_batches is None:
        out_shape = (t, nj * tn)
        out_map = lambda j, i, offs_ref: (i, j)
    else:
        seq = t // time_major_batches
        nt = seq // tm
        out_shape = (seq, nj * time_major_batches * tn)
        out_map = lambda j, i, offs_ref: (i % nt, j * time_major_batches + i // nt)

    def step_spec(spec):
        if spec is None:
            return pl.BlockSpec(memory_space=pltpu.SMEM)
        shape, index = spec
        return pl.BlockSpec(shape, lambda j, i, offs_ref: index(j * ni + i))

    side_in_specs = [] if side is None else [step_spec(sp) for sp in side.in_specs]
    side_out_specs = [] if side is None else [step_spec(sp) for sp in side.out_specs]
    side_inputs = () if side is None else side.inputs
    side_out_shapes = [] if side is None else list(side.out_shapes)
    side_scratch = [] if side is None else list(side.scratch)
    grid_spec = pltpu.PrefetchScalarGridSpec(
        num_scalar_prefetch=1,
        grid=(nj, ni),
        in_specs=[pl.BlockSpec((tm, k), lambda j, i, offs_ref: (i, 0)),
                  pl.BlockSpec((pl.Squeezed(), pl.Element(k), pl.Element(tn)),
                               lambda j, i, offs_ref: (layer, 0, offs_ref[j] * LANES))]
                 + side_in_specs,
        out_specs=[pl.BlockSpec((tm, tn), out_map)] + side_out_specs,
        scratch_shapes=[pltpu.VMEM((k, tn), bf16)] + side_scratch)
    res = pl.pallas_call(
        functools.partial(_proj_kernel, side_body=None if side is None else side.body,
                          n_side_in=len(side_in_specs), n_side_out=len(side_out_specs)),
        grid_spec=grid_spec,
        out_shape=[jax.ShapeDtypeStruct(out_shape, bf16)] + side_out_shapes,
        compiler_params=_cparams("arbitrary", "arbitrary"),
        name=name,
    )(offs, h, w_all, *side_inputs)
    return res[0] if side is None else res


def _s5_kernel(u_ref, z_ref, w2_ref, lam2_ref, cm_ref, d_ref, gw_ref, gb_ref, o_ref,
               ubuf, bu2_ref, st2_ref, y_ref, carry_ref):
    i = pl.program_id(0)
    steps = u_ref.shape[0]
    bsz = SUBLANES // 2
    rows = steps * bsz
    w = SSM_WIDTH

    @pl.when(i == 0)
    def _():
        carry_ref[...] = jnp.zeros_like(carry_ref)
        ubuf[:, 0:SUBLANES, :] = jnp.zeros((S5_NGB, SUBLANES, LANES), f32)

    @pl.when(i > 0)
    def _():
        ubuf[:, 0:SUBLANES, :] = ubuf[:, rows:rows + SUBLANES, :]

    for g in range(S5_NGB):
        cs = slice(g * LANES, (g + 1) * LANES)
        for b in range(bsz):
            ubuf[g, pl.ds(SUBLANES + b, steps, stride=bsz), :] = (
                u_ref[:, b * w + g * LANES:b * w + (g + 1) * LANES].astype(f32))
        lhs = jnp.concatenate(
            [ubuf[g, SUBLANES:rows + SUBLANES, :].astype(bf16),
             ubuf[g, SUBLANES - bsz:rows + SUBLANES - bsz, :].astype(bf16)], axis=1)
        bu_ref = bu2_ref.at[g % 2]
        st_ref = st2_ref.at[g % 2]
        bu_ref[...] = jnp.dot(lhs, w2_ref[g], preferred_element_type=f32)
        l_re = lam2_ref[g, :, :S5_HALF]
        l_im = lam2_ref[g, :, S5_HALF:]

        def body(r, s, l_re=l_re, l_im=l_im):
            s_re, s_im = s
            off = pl.multiple_of(r * SUBLANES, SUBLANES)
            b = bu_ref[pl.ds(off, SUBLANES), :]
            n_re = l_re * s_re - l_im * s_im + b[:, :S5_HALF]
            n_im = l_re * s_im + l_im * s_re + b[:, S5_HALF:]
            st_ref[pl.ds(off, SUBLANES), :S5_HALF] = n_re
            st_ref[pl.ds(off, SUBLANES), S5_HALF:] = n_im
            return n_re, n_im

        s0 = (carry_ref[g, :, :S5_HALF], carry_ref[g, :, S5_HALF:])
        s_re, s_im = lax.fori_loop(0, rows // SUBLANES, body, s0, unroll=True)
        carry_ref[g, :, :S5_HALF] = s_re
        carry_ref[g, :, S5_HALF:] = s_im
        y_g = jnp.dot(st_ref[...].astype(bf16), cm_ref[g], preferred_element_type=f32)
        y_ref[g] = y_g + d_ref[:, cs] * ubuf[g, SUBLANES:rows + SUBLANES, :]

    y = jax.nn.gelu(jnp.concatenate([y_ref[g] for g in range(S5_NGB)], axis=1))
    gate = jnp.dot(y.astype(bf16), gw_ref[...], preferred_element_type=f32) + gb_ref[...]
    y = y * jax.nn.sigmoid(gate)
    for g in range(S5_NGB):
        y_ref[g] = y[:, g * LANES:(g + 1) * LANES]
    for b in range(bsz):
        for g in range(S5_NGB):
            cs = slice(b * w + g * LANES, b * w + (g + 1) * LANES)
            z = z_ref[:, cs].astype(f32)
            o_ref[:, cs] = (y_ref[g, pl.ds(b, steps, stride=bsz), :]
                            * _silu(z)).astype(o_ref.dtype)


def _s5_params(a_re, a_im, log_dt, b_re, b_im, c_re, c_im):
    dt = jnp.exp(log_dt)[:, None]
    mag = jnp.exp(a_re * dt)
    lb_re = mag * jnp.cos(a_im * dt)
    lb_im = mag * jnp.sin(a_im * dt)
    den = a_re * a_re + a_im * a_im
    k_re = ((lb_re - 1.0) * a_re + lb_im * a_im) / den
    k_im = (lb_im * a_re - (lb_re - 1.0) * a_im) / den
    bb_re = k_re[..., None] * b_re - k_im[..., None] * b_im
    bb_im = k_re[..., None] * b_im + k_im[..., None] * b_re
    t_re = lb_re[..., None] * bb_re - lb_im[..., None] * bb_im
    t_im = lb_re[..., None] * bb_im + lb_im[..., None] * bb_re

    def block_diag(m, rows_per_group, cols_per_group):
        m = m.reshape(S5_NGB, S5_GB * rows_per_group, cols_per_group)
        tiled = jnp.concatenate([m] * S5_GB, axis=2)
        r = lax.broadcasted_iota(jnp.int32, tiled.shape, 1) // rows_per_group
        c = lax.broadcasted_iota(jnp.int32, tiled.shape, 2) // cols_per_group
        return jnp.where(r == c, tiled, 0.0)

    in_block = lambda m: block_diag(m.transpose(0, 2, 1), SSM_GROUP, SSM_STATE)
    out_block = lambda m: block_diag(m.transpose(0, 2, 1), SSM_STATE, SSM_GROUP)
    w2 = jnp.concatenate([
        jnp.concatenate([in_block(bb_re), in_block(bb_im)], axis=2),
        jnp.concatenate([in_block(t_re), in_block(t_im)], axis=2)], axis=1)
    cm = jnp.concatenate([out_block(c_re), out_block(-c_im)], axis=1)
    l2_re = (lb_re * lb_re - lb_im * lb_im).reshape(S5_NGB, 1, S5_HALF)
    l2_im = (2.0 * lb_re * lb_im).reshape(S5_NGB, 1, S5_HALF)
    lam2 = jnp.broadcast_to(jnp.concatenate([l2_re, l2_im], axis=2),
                            (S5_NGB, SUBLANES, S5_SW))
    return w2.astype(bf16), lam2, cm.astype(bf16)


def _s5_mixer(uz, w2, lam2, cm, d, glu_w, glu_b, bsz, steps=128):
    seq = uz.shape[0]
    w = SSM_WIDTH
    bw = bsz * w
    rows = steps * bsz
    const3 = lambda i: (0, 0, 0)
    const2 = lambda i: (0, 0)
    return pl.pallas_call(
        _s5_kernel,
        grid=(seq // steps,),
        in_specs=[pl.BlockSpec((steps, bw), lambda i: (i, 0)),
                  pl.BlockSpec((steps, bw), lambda i: (i, 1)),
                  pl.BlockSpec(w2.shape, const3),
                  pl.BlockSpec(lam2.shape, const3),
                  pl.BlockSpec(cm.shape, const3),
                  pl.BlockSpec((1, w), const2),
                  pl.BlockSpec((w, w), const2),
                  pl.BlockSpec((1, w), const2)],
        out_specs=pl.BlockSpec((steps, bw), lambda i: (i, 0)),
        out_shape=jax.ShapeDtypeStruct((seq, bw), bf16),
        scratch_shapes=[pltpu.VMEM((S5_NGB, rows + SUBLANES, LANES), f32),
                        pltpu.VMEM((2, rows, S5_SW), f32),
                        pltpu.VMEM((2, rows, S5_SW), f32),
                        pltpu.VMEM((S5_NGB, rows, LANES), f32),
                        pltpu.VMEM((S5_NGB, SUBLANES, S5_SW), f32)],
        compiler_params=_cparams("arbitrary"),
        name="s5_mixer",
    )(uz, uz, w2, lam2, cm, d.reshape(1, w), glu_w, glu_b.reshape(1, w))


def _cast_body(n_arrays, step, *refs):
    del step

    def cast():
        for src, dst in zip(refs[:n_arrays], refs[n_arrays:]):
            dst[...] = src[...].astype(dst.dtype)

    return [cast]


def _cast_job(arrays, layer, steps, n_blocks=32):
    block = lambda step: (jnp.minimum(step, n_blocks - 1), 0)
    specs, shapes = [], []
    for m in arrays:
        rows = m.shape[1] // n_blocks
        assert rows * n_blocks == m.shape[1] and rows % (2 * SUBLANES) == 0
        specs.append(((pl.Squeezed(), rows, m.shape[2]),
                      lambda step: (layer, jnp.minimum(step, n_blocks - 1), 0)))
        shapes.append(((rows, m.shape[2]), jax.ShapeDtypeStruct(m.shape[1:], bf16)))
    return _SideJob(
        steps=steps,
        body=functools.partial(_cast_body, len(arrays)),
        inputs=tuple(arrays),
        in_specs=tuple(specs),
        out_shapes=tuple(sh for _, sh in shapes),
        out_specs=tuple((blk, block) for blk, _ in shapes),
        scratch=())


def _gmlp_body(step, u_ref, v_ref, z_ref, lnw_ref, lnb_ref, ws_ref, bias_ref, o_ref,
               vn_ref, wm_ref):
    del step
    rows = u_ref.shape[0]

    def norm_part(c):
        rs = slice(c * SG_CHUNK, (c + 1) * SG_CHUNK)
        if c == 0:
            t_idx = lax.broadcasted_iota(jnp.int32, (SG_CHUNK, SG_CHUNK), 0)
            s_idx = lax.broadcasted_iota(jnp.int32, (SG_CHUNK, SG_CHUNK), 1)
            for h in range(SG_HEADS):
                wm_ref[h] = jnp.where(s_idx <= t_idx, ws_ref[h], 0.0).astype(bf16)
        v = jax.nn.gelu(v_ref[rs, :].astype(f32))
        mu = jnp.mean(v, axis=-1, keepdims=True)
        vc = v - mu
        var = jnp.mean(vc * vc, axis=-1, keepdims=True)
        vn_ref[rs, :] = (vc * lax.rsqrt(var + EPS) * lnw_ref[...] + lnb_ref[...]).astype(bf16)

    def mix_part(c, h):
        rs = slice(c * SG_CHUNK, (c + 1) * SG_CHUNK)
        cs = slice(h * LANES, (h + 1) * LANES)
        mixed = jnp.dot(wm_ref[h], vn_ref[rs, cs], preferred_element_type=f32)
        mixed = mixed + bias_ref[:, cs]
        u = jax.nn.gelu(u_ref[rs, cs].astype(f32))
        z = z_ref[rs, cs].astype(f32)
        o_ref[rs, cs] = (u * mixed * _silu(z)).astype(o_ref.dtype)

    parts = []
    for c in range(rows // SG_CHUNK):
        parts.append(functools.partial(norm_part, c))
        parts.extend(functools.partial(mix_part, c, h) for h in range(SG_HEADS))
    return parts


def _gmlp_job(p, ln_w, ln_b, w_s, b_s, rows=512):
    t = p.shape[0]
    w = SG_WIDTH
    bias = jnp.repeat(b_s.T, LANES, axis=1)
    const2 = lambda step: (0, 0)
    return _SideJob(
        steps=t // rows,
        body=_gmlp_body,
        inputs=(p, p, p, ln_w.reshape(1, w), ln_b.reshape(1, w), w_s, bias),
        in_specs=(((rows, w), lambda step: (step, 0)),
                  ((rows, w), lambda step: (step, 1)),
                  ((rows, w), lambda step: (step, 2)),
                  ((1, w), const2),
                  ((1, w), const2),
                  (w_s.shape, lambda step: (0, 0, 0)),
                  ((SG_CHUNK, w), const2)),
        out_shapes=(jax.ShapeDtypeStruct((t, w), bf16),),
        out_specs=(((rows, w), lambda step: (step, 0)),),
        scratch=(pltpu.VMEM((rows, w), bf16),
                 pltpu.VMEM((SG_HEADS, SG_CHUNK, SG_CHUNK), bf16)))


def _rope(x, cos, sin_up, sin_dn):
    half = ROT_DIM // 2
    return (x * cos + pltpu.roll(x, LANES - half, 1) * sin_up
            + pltpu.roll(x, half, 1) * sin_dn)


def _attn_body(nb, step, sink_ref, q_ref, z_ref, kvp_ref, kvc_ref, rp_ref, rc_ref, o_ref):
    n = step % nb
    blk = ATT_BLOCK
    group = ATT_HEADS // ATT_KV_HEADS
    pre = {}

    def prepare():
        cos_c, up_c, dn_c = rc_ref[0], rc_ref[1], rc_ref[2]
        cos_p, up_p, dn_p = rp_ref[0], rp_ref[1], rp_ref[2]
        k_all = jnp.concatenate([
            _rope(kvp_ref[:, :LANES].astype(f32), cos_p, up_p, dn_p),
            _rope(kvc_ref[:, :LANES].astype(f32), cos_c, up_c, dn_c)], axis=0)
        v_all = jnp.concatenate([kvp_ref[:, LANES:].astype(f32),
                                 kvc_ref[:, LANES:].astype(f32)], axis=0)
        lane = lax.broadcasted_iota(jnp.int32, (2 * blk, LANES), 1)
        low = lane < HEAD_DIM
        k_sw = pltpu.roll(k_all, HEAD_DIM, 1)
        v_sw = pltpu.roll(v_all, HEAD_DIM, 1)
        pre["k"] = [[jnp.where(low, k_all, 0.0).astype(bf16),
                     jnp.where(low, 0.0, k_sw).astype(bf16)],
                    [jnp.where(low, k_sw, 0.0).astype(bf16),
                     jnp.where(low, 0.0, k_all).astype(bf16)]]
        pre["v"] = [[jnp.where(low, v_all, 0.0).astype(bf16),
                     jnp.where(low, 0.0, v_sw).astype(bf16)],
                    [jnp.where(low, v_sw, 0.0).astype(bf16),
                     jnp.where(low, 0.0, v_all).astype(bf16)]]
        qi = lax.broadcasted_iota(jnp.int32, (blk, blk), 0)
        kj = lax.broadcasted_iota(jnp.int32, (blk, blk), 1)
        pre["in_cur"] = kj <= qi
        pre["prev_bias"] = jnp.where(pre["in_cur"] | (n > 0), 0.0, NEG_INF)
        scale = HEAD_DIM ** -0.5
        pre["rope_q"] = (cos_c * scale, up_c * scale, dn_c * scale)

    def staged(tile_lo, tile_hi):
        heads = range(2 * tile_lo, 2 * tile_hi)
        tiles = {j: slice(j * LANES, (j + 1) * LANES) for j in range(tile_lo, tile_hi)}
        st = {}

        def scores():
            qs = {j: _rope(q_ref[:, cs].astype(f32), *pre["rope_q"]).astype(bf16)
                  for j, cs in tiles.items()}
            s2 = {hd: lax.dot_general(qs[hd // 2], pre["k"][hd // group][hd % 2],
                                      (((1,), (1,)), ((), ())), preferred_element_type=f32)
                  for hd in heads}
            st["s"] = {hd: jnp.where(pre["in_cur"], s2[hd][:, blk:], s2[hd][:, :blk])
                       + pre["prev_bias"] for hd in heads}

        def row_max():
            st["m"] = {hd: jnp.maximum(jnp.max(st["s"][hd], axis=-1, keepdims=True), sink_ref[hd])
                       for hd in heads}

        def exponent():
            st["p"] = {hd: jnp.exp(st["s"][hd] - st["m"][hd]) for hd in heads}
            st["den"] = {hd: jnp.sum(st["p"][hd], axis=-1, keepdims=True)
                         + jnp.exp(sink_ref[hd] - st["m"][hd]) for hd in heads}

        def values():
            p, in_cur = st["p"], pre["in_cur"]
            p2 = {hd: jnp.concatenate([jnp.where(in_cur, 0.0, p[hd]),
                                       jnp.where(in_cur, p[hd], 0.0)], axis=1).astype(bf16)
                  for hd in heads}
            st["pv"] = {hd: jnp.dot(p2[hd], pre["v"][hd // group][hd % 2],
                                    preferred_element_type=f32) * (1.0 / st["den"][hd])
                        for hd in heads}

        def output():
            pv = st["pv"]
            for j, cs in tiles.items():
                z = z_ref[:, cs].astype(f32)
                o_ref[:, cs] = ((pv[2 * j] + pv[2 * j + 1]) * _silu(z)).astype(o_ref.dtype)

        return [scores, row_max, exponent, values, output]

    n_tiles = ATT_HEADS // 2
    bounds = [0, 3, 6, n_tiles]
    return [prepare] + [stage for lo, hi in zip(bounds[:-1], bounds[1:])
                        for stage in staged(lo, hi)]


def _rope_tables(seq):
    half = ROT_DIM // 2
    inv_freq = ROPE_THETA ** (-jnp.arange(0, ROT_DIM, 2, dtype=f32) / ROT_DIM)
    ang = jnp.arange(seq, dtype=f32)[:, None] * inv_freq[None, :]
    cos, sin = jnp.cos(ang), jnp.sin(ang)
    ones = jnp.ones((seq, HEAD_DIM - ROT_DIM), f32)
    zeros = jnp.zeros((seq, HEAD_DIM - ROT_DIM), f32)
    zh = jnp.zeros((seq, half), f32)
    cos_t = jnp.concatenate([cos, cos, ones], axis=1)
    up_t = jnp.concatenate([-sin, zh, zeros], axis=1)
    dn_t = jnp.concatenate([zh, sin, zeros], axis=1)
    tab = jnp.stack([cos_t, up_t, dn_t])
    return jnp.concatenate([tab, tab], axis=2)


def _attn_job(p, kv, sinks, rope, bsz, seq):
    nb = seq // ATT_BLOCK
    w = ATT_WIDTH
    prev = lambda step: step - jnp.minimum(step % nb, 1)
    return _SideJob(
        steps=bsz * nb,
        body=functools.partial(_attn_body, nb),
        inputs=(sinks, p, p, kv, kv, rope, rope),
        in_specs=(None,
                  ((ATT_BLOCK, w), lambda step: (step, 3)),
                  ((ATT_BLOCK, w), lambda step: (step, 4)),
                  ((ATT_BLOCK, 2 * KV_WIDTH), lambda step: (prev(step), 0)),
                  ((ATT_BLOCK, 2 * KV_WIDTH), lambda step: (step, 0)),
                  ((3, ATT_BLOCK, LANES), lambda step: (0, prev(step) % nb, 0)),
                  ((3, ATT_BLOCK, LANES), lambda step: (0, step % nb, 0))),
        out_shapes=(jax.ShapeDtypeStruct((bsz * seq, w), bf16),),
        out_specs=(((ATT_BLOCK, w), lambda step: (step, 0)),),
        scratch=())


def _merge_kernel(ya_ref, yb_ref, yc_ref, g_ref, x_ref, wa_ref, wb_ref, wc_ref, wo_ref,
                  nw_ref, *out_refs, last, chunk):
    merged_ref = out_refs[-1]
    ys = (ya_ref, yb_ref, yc_ref)
    ws = (wa_ref, wb_ref, wc_ref)
    for c in range(D_MODEL // chunk):
        cs = slice(c * chunk, (c + 1) * chunk)
        acc = None
        for k in range(N_BRANCH):
            gate = jax.nn.sigmoid(g_ref[:, k * D_MODEL + c * chunk:k * D_MODEL + (c + 1) * chunk]
                                  .astype(f32))
            term = gate * jnp.dot(ys[k][...], ws[k][:, cs], preferred_element_type=f32)
            acc = term if acc is None else acc + term
        merged_ref[:, cs] = acc.astype(bf16)
    x_new = x_ref[...] + jnp.dot(merged_ref[...], wo_ref[...], preferred_element_type=f32)
    ms = jnp.mean(x_new * x_new, axis=-1, keepdims=True)
    normed = x_new * lax.rsqrt(ms + EPS) * nw_ref[...]
    if last:
        out_refs[0][...] = normed
    else:
        out_refs[0][...] = x_new
        out_refs[1][...] = normed.astype(bf16)


def _merge_out(ya_tm, yb, yc, gates, x, wa, wb, wc, wo, next_norm_w, bsz, seq, last, tm=256):
    t = bsz * seq
    nt = seq // tm
    d = D_MODEL
    row = lambda b, n: (b * nt + n, 0)
    const = lambda b, n: (0, 0)
    once = pl.Buffered(1)
    wspec = lambda m: pl.BlockSpec(m.shape, const, pipeline_mode=once)
    if last:
        out_shape = [jax.ShapeDtypeStruct((t, d), f32)]
        out_specs = [pl.BlockSpec((tm, d), row)]
    else:
        out_shape = [jax.ShapeDtypeStruct((t, d), f32), jax.ShapeDtypeStruct((t, d), bf16)]
        out_specs = [pl.BlockSpec((tm, d), row), pl.BlockSpec((tm, d), row)]
    return pl.pallas_call(
        functools.partial(_merge_kernel, last=last, chunk=512),
        grid=(bsz, nt),
        in_specs=[pl.BlockSpec((tm, SSM_WIDTH), lambda b, n: (n, b)),
                  pl.BlockSpec((tm, SG_WIDTH), row),
                  pl.BlockSpec((tm, ATT_WIDTH), row),
                  pl.BlockSpec((tm, N_BRANCH * d), row),
                  pl.BlockSpec((tm, d), row),
                  wspec(wa), wspec(wb), wspec(wc), wspec(wo),
                  pl.BlockSpec((1, d), const)],
        out_specs=out_specs,
        out_shape=out_shape,
        scratch_shapes=[pltpu.VMEM((tm, d), bf16)],
        compiler_params=_cparams("parallel", "parallel"),
        name="merge_out",
    )(ya_tm, yb, yc, gates, x, wa, wb, wc, wo, next_norm_w.reshape(1, d))


def kernel(x, norm_w, w_in, ssm_a_re, ssm_a_im, ssm_log_dt, ssm_b_re, ssm_b_im, ssm_c_re, ssm_c_im, ssm_d, ssm_glu_w, ssm_glu_b, sg_ln_w, sg_ln_b, sg_w, sg_b, attn_sinks, w_branch_a, w_branch_b, w_branch_c, w_out, final_norm_w):
    bsz, seq, d = x.shape
    assert 2 * bsz == SUBLANES, "the S5 kernel packs two timesteps of all batches per sublane tile"
    depth = norm_w.shape[0]
    t = bsz * seq
    xf = x.reshape(t, d)
    rope = _rope_tables(seq)

    o_ub = 2 * SSM_WIDTH
    o_q = o_ub + 3 * SG_WIDTH
    o_k = o_q + ATT_WIDTH
    o_zc = o_k + 2 * KV_WIDTH
    o_g = o_zc + ATT_WIDTH

    tn = 1024
    gate_tn = 768
    s5_cols = [0, SSM_WIDTH]
    bc_cols = [o_ub, o_ub + tn, o_ub + 2 * tn, o_q, o_zc]
    gate_cols = [o_g + c * gate_tn for c in range(N_BRANCH * d // gate_tn)]

    w2s, lam2s, cms = jax.vmap(_s5_params)(ssm_a_re, ssm_a_im, ssm_log_dt, ssm_b_re,
                                           ssm_b_im, ssm_c_re, ssm_c_im)
    layer_weights = (ssm_glu_w, w_branch_a, w_branch_b, w_branch_c, w_out)

    h = _rmsnorm(xf, norm_w[0], bf16)
    out = None
    for l in range(depth):
        p_bc, gw, wa, wb, wc, wo = _project(
            h, w_in, l, bc_cols, tn,
            side=_cast_job(layer_weights, l, steps=len(bc_cols) * (t // 1024)), name="proj_bc")
        kv = _project(h, w_in, l, [o_k], 2 * KV_WIDTH, name="proj_kv")
        uz, yb = _project(h, w_in, l, s5_cols, tn, time_major_batches=bsz,
                          side=_gmlp_job(p_bc, sg_ln_w[l], sg_ln_b[l], sg_w[l], sg_b[l]),
                          name="proj_s5_gmlp")
        gates, yc = _project(h, w_in, l, gate_cols, gate_tn,
                             side=_attn_job(p_bc, kv, attn_sinks[l], rope, bsz, seq),
                             name="proj_gates_attn")
        ya = _s5_mixer(uz, w2s[l], lam2s[l], cms[l], ssm_d[l], gw, ssm_glu_b[l], bsz)

        last = l == depth - 1
        next_w = final_norm_w if last else norm_w[l + 1]
        res = _merge_out(ya, yb, yc, gates, xf, wa, wb, wc, wo, next_w, bsz, seq, last)
        if last:
            out = res[0]
        else:
            xf, h = res
    return out.reshape(bsz, seq, d)
```

```python
import functools
import math
from typing import Callable, NamedTuple

import jax
import jax.numpy as jnp
from jax import lax
from jax.experimental import pallas as pl
from jax.experimental.pallas import tpu as pltpu

f32 = jnp.float32
bf16 = jnp.bfloat16

D_MODEL = 2048
EPS = 1e-6
NEG_INF = -1e30

SSM_WIDTH = D_MODEL // 2
SSM_GROUP = 16
SSM_GROUPS = SSM_WIDTH // SSM_GROUP
SSM_STATE = 64

SG_WIDTH = D_MODEL // 2
SG_HEADS = 8
SG_CHUNK = 128

HEAD_DIM = 64
ATT_HEADS = D_MODEL // 128
ATT_KV_HEADS = ATT_HEADS // 8
ATT_WIDTH = ATT_HEADS * HEAD_DIM
KV_WIDTH = ATT_KV_HEADS * HEAD_DIM
ATT_BLOCK = 128
ROT_DIM = HEAD_DIM // 4
ROPE_THETA = 500000.0
N_BRANCH = 3

LANES = 128
SUBLANES = 8
MXU_DIM = 256
VMEM_LIMIT_BYTES = 48 * 1024 * 1024

S5_GB = LANES // SSM_GROUP
S5_NGB = SSM_GROUPS // S5_GB
S5_HALF = S5_GB * SSM_STATE
S5_SW = 2 * S5_HALF


def _cparams(*sem):
    return pltpu.CompilerParams(dimension_semantics=sem, vmem_limit_bytes=VMEM_LIMIT_BYTES)


def _silu(x):
    return x * jax.nn.sigmoid(x)


def _rmsnorm_kernel(x_ref, w_ref, o_ref):
    x = x_ref[...]
    ms = jnp.mean(x * x, axis=-1, keepdims=True)
    o_ref[...] = (x * lax.rsqrt(ms + EPS) * w_ref[...]).astype(o_ref.dtype)


def _rmsnorm(x, w, out_dtype, tm=512):
    t, d = x.shape
    return pl.pallas_call(
        _rmsnorm_kernel,
        grid=(t // tm,),
        in_specs=[pl.BlockSpec((tm, d), lambda i: (i, 0)),
                  pl.BlockSpec((1, d), lambda i: (0, 0))],
        out_specs=pl.BlockSpec((tm, d), lambda i: (i, 0)),
        out_shape=jax.ShapeDtypeStruct((t, d), out_dtype),
        compiler_params=_cparams("parallel"),
        name="rmsnorm",
    )(x, w.reshape(1, d))


class _SideJob(NamedTuple):
    steps: int
    body: Callable
    inputs: tuple
    in_specs: tuple
    out_shapes: tuple
    out_specs: tuple
    scratch: tuple


def _proj_kernel(offs_ref, h_ref, w_ref, *refs, side_body, n_side_in, n_side_out):
    del offs_ref
    side_in = refs[:n_side_in]
    o_ref = refs[n_side_in]
    side_out = refs[n_side_in + 1:n_side_in + 1 + n_side_out]
    wbf_ref = refs[n_side_in + 1 + n_side_out]
    side_scratch = refs[n_side_in + 2 + n_side_out:]

    @pl.when(pl.program_id(1) == 0)
    def _():
        wbf_ref[...] = w_ref[...].astype(bf16)

    parts = []
    if side_body is not None:
        step = pl.program_id(0) * pl.num_programs(1) + pl.program_id(1)
        parts = side_body(step, *side_in, *side_out, *side_scratch)
    tm, tn = o_ref.shape
    if not parts:
        o_ref[...] = jnp.dot(h_ref[...], wbf_ref[...],
                             preferred_element_type=f32).astype(o_ref.dtype)
        return
    pieces = [(rs, cs) for cs in range(0, tn, MXU_DIM) for rs in range(0, tm, MXU_DIM)]
    n_parts, n_gaps = len(parts), len(pieces) - 1
    done = 0
    for k, (rs, cs) in enumerate(pieces):
        o_ref[rs:rs + MXU_DIM, cs:cs + MXU_DIM] = jnp.dot(
            h_ref[rs:rs + MXU_DIM, :], wbf_ref[:, cs:cs + MXU_DIM],
            preferred_element_type=f32).astype(o_ref.dtype)
        upto = min(n_parts, -(-(k + 1) * n_parts // n_gaps))
        for part in parts[done:upto]:
            part()
        done = upto


def _project(h, w_all, layer, col_offsets, tn, tm=1024, time_major_batches=None, side=None,
             name="proj"):
    t, k = h.shape
    nj = len(col_offsets)
    ni = t // tm
    assert all(off % LANES == 0 for off in col_offsets)
    assert side is None or side.steps == nj * ni
    offs = jnp.asarray([off // LANES for off in col_offsets], jnp.int32)
    if time_major_batches is None:
        out_shape = (t, nj * tn)
        out_map = lambda j, i, offs_ref: (i, j)
    else:
        seq = t // time_major_batches
        nt = seq // tm
        out_shape = (seq, nj * time_major_batches * tn)
        out_map = lambda j, i, offs_ref: (i % nt, j * time_major_batches + i // nt)

    def step_spec(spec):
        if spec is None:
            return pl.BlockSpec(memory_space=pltpu.SMEM)
        shape, index = spec
        return pl.BlockSpec(shape, lambda j, i, offs_ref: index(j * ni + i))

    side_in_specs = [] if side is None else [step_spec(sp) for sp in side.in_specs]
    side_out_specs = [] if side is None else [step_spec(sp) for sp in side.out_specs]
    side_inputs = () if side is None else side.inputs
    side_out_shapes = [] if side is None else list(side.out_shapes)
    side_scratch = [] if side is None else list(side.scratch)
    grid_spec = pltpu.PrefetchScalarGridSpec(
        num_scalar_prefetch=1,
        grid=(nj, ni),
        in_specs=[pl.BlockSpec((tm, k), lambda j, i, offs_ref: (i, 0)),
                  pl.BlockSpec((pl.Squeezed(), pl.Element(k), pl.Element(tn)),
                               lambda j, i, offs_ref: (layer, 0, offs_ref[j] * LANES))]
                 + side_in_specs,
        out_specs=[pl.BlockSpec((tm, tn), out_map)] + side_out_specs,
        scratch_shapes=[pltpu.VMEM((k, tn), bf16)] + side_scratch)
    res = pl.pallas_call(
        functools.partial(_proj_kernel, side_body=None if side is None else side.body,
                          n_side_in=len(side_in_specs), n_side_out=len(side_out_specs)),
        grid_spec=grid_spec,
        out_shape=[jax.ShapeDtypeStruct(out_shape, bf16)] + side_out_shapes,
        compiler_params=_cparams("arbitrary", "arbitrary"),
        name=name,
    )(offs, h, w_all, *side_inputs)
    return res[0] if side is None else res


def _tile_lanes(x, width):
    while width < LANES:
        x = x + pltpu.roll(x, width, 1)
        width *= 2
    return x


def _s5_kernel(u_ref, z_ref, b4_ref, lam2_ref, c2_ref, d_ref, gw_ref, gb_ref, o_ref,
               w2_ref, cm_ref, ubuf, bu2_ref, st2_ref, y_ref, carry_ref):
    i = pl.program_id(0)
    steps = u_ref.shape[0]
    bsz = SUBLANES // 2
    rows = steps * bsz
    w = SSM_WIDTH

    @pl.when(i == 0)
    def _():
        carry_ref[...] = jnp.zeros_like(carry_ref)
        ubuf[:, 0:SUBLANES, :] = jnp.zeros((S5_NGB, SUBLANES, LANES), f32)
        in_diag = (lax.broadcasted_iota(jnp.int32, (LANES, S5_HALF), 0) // SSM_GROUP
                   == lax.broadcasted_iota(jnp.int32, (LANES, S5_HALF), 1) // SSM_STATE)
        out_diag = (lax.broadcasted_iota(jnp.int32, (S5_HALF, LANES), 0) // SSM_STATE
                    == lax.broadcasted_iota(jnp.int32, (S5_HALF, LANES), 1) // SSM_GROUP)
        for g in range(S5_NGB):
            for k in range(4):
                x = _tile_lanes(b4_ref[k, g], SSM_STATE)
                x = jnp.concatenate([x] * (S5_HALF // LANES), axis=1)
                w2_ref[g, (k // 2) * LANES:(k // 2 + 1) * LANES,
                       (k % 2) * S5_HALF:(k % 2 + 1) * S5_HALF] = (
                           jnp.where(in_diag, x, 0.0).astype(bf16))
            for k in range(2):
                x = _tile_lanes(c2_ref[k, g], SSM_GROUP)
                cm_ref[g, k * S5_HALF:(k + 1) * S5_HALF, :] = (
                    jnp.where(out_diag, x, 0.0).astype(bf16))

    @pl.when(i > 0)
    def _():
        ubuf[:, 0:SUBLANES, :] = ubuf[:, rows:rows + SUBLANES, :]

    for g in range(S5_NGB):
        cs = slice(g * LANES, (g + 1) * LANES)
        for b in range(bsz):
            ubuf[g, pl.ds(SUBLANES + b, steps, stride=bsz), :] = (
                u_ref[:, b * w + g * LANES:b * w + (g + 1) * LANES].astype(f32))
        lhs = jnp.concatenate(
            [ubuf[g, SUBLANES:rows + SUBLANES, :].astype(bf16),
             ubuf[g, SUBLANES - bsz:rows + SUBLANES - bsz, :].astype(bf16)], axis=1)
        bu_ref = bu2_ref.at[g % 2]
        st_ref = st2_ref.at[g % 2]
        bu_ref[...] = jnp.dot(lhs, w2_ref[g], preferred_element_type=f32)
        l_re = lam2_ref[g, :, :S5_HALF]
        l_im = lam2_ref[g, :, S5_HALF:]

        def body(r, s, l_re=l_re, l_im=l_im):
            s_re, s_im = s
            off = pl.multiple_of(r * SUBLANES, SUBLANES)
            b = bu_ref[pl.ds(off, SUBLANES), :]
            n_re = l_re * s_re - l_im * s_im + b[:, :S5_HALF]
            n_im = l_re * s_im + l_im * s_re + b[:, S5_HALF:]
            st_ref[pl.ds(off, SUBLANES), :S5_HALF] = n_re
            st_ref[pl.ds(off, SUBLANES), S5_HALF:] = n_im
            return n_re, n_im

        s0 = (carry_ref[g, :, :S5_HALF], carry_ref[g, :, S5_HALF:])
        s_re, s_im = lax.fori_loop(0, rows // SUBLANES, body, s0, unroll=True)
        carry_ref[g, :, :S5_HALF] = s_re
        carry_ref[g, :, S5_HALF:] = s_im
        y_g = jnp.dot(st_ref[...].astype(bf16), cm_ref[g], preferred_element_type=f32)
        y_ref[g] = y_g + d_ref[:, cs] * ubuf[g, SUBLANES:rows + SUBLANES, :]

    y = jax.nn.gelu(jnp.concatenate([y_ref[g] for g in range(S5_NGB)], axis=1))
    gate = jnp.dot(y.astype(bf16), gw_ref[...], preferred_element_type=f32) + gb_ref[...]
    y = y * jax.nn.sigmoid(gate)
    for g in range(S5_NGB):
        y_ref[g] = y[:, g * LANES:(g + 1) * LANES]
    for b in range(bsz):
        for g in range(S5_NGB):
            cs = slice(b * w + g * LANES, b * w + (g + 1) * LANES)
            z = z_ref[:, cs].astype(f32)
            o_ref[:, cs] = (y_ref[g, pl.ds(b, steps, stride=bsz), :]
                            * _silu(z)).astype(o_ref.dtype)


def _s5_params(a_re, a_im, log_dt, b_re, b_im, c_re, c_im):
    dt = jnp.exp(log_dt)[:, None]
    mag = jnp.exp(a_re * dt)
    lb_re = mag * jnp.cos(a_im * dt)
    lb_im = mag * jnp.sin(a_im * dt)
    den = a_re * a_re + a_im * a_im
    k_re = ((lb_re - 1.0) * a_re + lb_im * a_im) / den
    k_im = (lb_im * a_re - (lb_re - 1.0) * a_im) / den
    bb_re = k_re[..., None] * b_re - k_im[..., None] * b_im
    bb_im = k_re[..., None] * b_im + k_im[..., None] * b_re
    t_re = lb_re[..., None] * bb_re - lb_im[..., None] * bb_im
    t_im = lb_re[..., None] * bb_im + lb_im[..., None] * bb_re

    def pad_lanes(m):
        return jnp.pad(m, ((0, 0), (0, 0), (0, LANES - m.shape[-1])))

    b4 = jnp.stack([pad_lanes(m.transpose(0, 2, 1).reshape(S5_NGB, LANES, SSM_STATE))
                    for m in (bb_re, bb_im, t_re, t_im)])
    c2 = jnp.stack([pad_lanes(m.transpose(0, 2, 1).reshape(S5_NGB, S5_HALF, SSM_GROUP))
                    for m in (c_re, -c_im)])
    l2_re = (lb_re * lb_re - lb_im * lb_im).reshape(S5_NGB, 1, S5_HALF)
    l2_im = (2.0 * lb_re * lb_im).reshape(S5_NGB, 1, S5_HALF)
    lam2 = jnp.broadcast_to(jnp.concatenate([l2_re, l2_im], axis=2),
                            (S5_NGB, SUBLANES, S5_SW))
    return b4, lam2, c2


def _s5_mixer(uz, b4, lam2, c2, d, glu_w, glu_b, bsz, steps=128):
    seq = uz.shape[0]
    w = SSM_WIDTH
    bw = bsz * w
    rows = steps * bsz
    const4 = lambda i: (0, 0, 0, 0)
    const3 = lambda i: (0, 0, 0)
    const2 = lambda i: (0, 0)
    return pl.pallas_call(
        _s5_kernel,
        grid=(seq // steps,),
        in_specs=[pl.BlockSpec((steps, bw), lambda i: (i, 0)),
                  pl.BlockSpec((steps, bw), lambda i: (i, 1)),
                  pl.BlockSpec(b4.shape, const4),
                  pl.BlockSpec(lam2.shape, const3),
                  pl.BlockSpec(c2.shape, const4),
                  pl.BlockSpec((1, w), const2),
                  pl.BlockSpec((w, w), const2),
                  pl.BlockSpec((1, w), const2)],
        out_specs=pl.BlockSpec((steps, bw), lambda i: (i, 0)),
        out_shape=jax.ShapeDtypeStruct((seq, bw), bf16),
        scratch_shapes=[pltpu.VMEM((S5_NGB, 2 * LANES, S5_SW), bf16),
                        pltpu.VMEM((S5_NGB, S5_SW, LANES), bf16),
                        pltpu.VMEM((S5_NGB, rows + SUBLANES, LANES), f32),
                        pltpu.VMEM((2, rows, S5_SW), f32),
                        pltpu.VMEM((2, rows, S5_SW), f32),
                        pltpu.VMEM((S5_NGB, rows, LANES), f32),
                        pltpu.VMEM((S5_NGB, SUBLANES, S5_SW), f32)],
        compiler_params=_cparams("arbitrary"),
        name="s5_mixer",
    )(uz, uz, b4, lam2, c2, d.reshape(1, w), glu_w, glu_b.reshape(1, w))


def _cast_body(n_arrays, step, *refs):
    del step

    def cast():
        for src, dst in zip(refs[:n_arrays], refs[n_arrays:]):
            dst[...] = src[...].astype(dst.dtype)

    return [cast]


def _cast_job(arrays, layer, steps, n_blocks=32):
    block = lambda step: (jnp.minimum(step, n_blocks - 1), 0)
    specs, shapes = [], []
    for m in arrays:
        rows = m.shape[1] // n_blocks
        assert rows * n_blocks == m.shape[1] and rows % (2 * SUBLANES) == 0
        specs.append(((pl.Squeezed(), rows, m.shape[2]),
                      lambda step: (layer, jnp.minimum(step, n_blocks - 1), 0)))
        shapes.append(((rows, m.shape[2]), jax.ShapeDtypeStruct(m.shape[1:], bf16)))
    return _SideJob(
        steps=steps,
        body=functools.partial(_cast_body, len(arrays)),
        inputs=tuple(arrays),
        in_specs=tuple(specs),
        out_shapes=tuple(sh for _, sh in shapes),
        out_specs=tuple((blk, block) for blk, _ in shapes),
        scratch=())


def _gmlp_body(step, u_ref, v_ref, z_ref, lnw_ref, lnb_ref, ws_ref, bias_ref, o_ref,
               vn_ref, wm_ref):
    del step
    rows = u_ref.shape[0]

    def norm_part(c):
        rs = slice(c * SG_CHUNK, (c + 1) * SG_CHUNK)
        if c == 0:
            t_idx = lax.broadcasted_iota(jnp.int32, (SG_CHUNK, SG_CHUNK), 0)
            s_idx = lax.broadcasted_iota(jnp.int32, (SG_CHUNK, SG_CHUNK), 1)
            for h in range(SG_HEADS):
                wm_ref[h] = jnp.where(s_idx <= t_idx, ws_ref[h], 0.0).astype(bf16)
        v = jax.nn.gelu(v_ref[rs, :].astype(f32))
        mu = jnp.mean(v, axis=-1, keepdims=True)
        vc = v - mu
        var = jnp.mean(vc * vc, axis=-1, keepdims=True)
        vn_ref[rs, :] = (vc * lax.rsqrt(var + EPS) * lnw_ref[...] + lnb_ref[...]).astype(bf16)

    def mix_part(c, h):
        rs = slice(c * SG_CHUNK, (c + 1) * SG_CHUNK)
        cs = slice(h * LANES, (h + 1) * LANES)
        mixed = jnp.dot(wm_ref[h], vn_ref[rs, cs], preferred_element_type=f32)
        mixed = mixed + bias_ref[:, cs]
        u = jax.nn.gelu(u_ref[rs, cs].astype(f32))
        z = z_ref[rs, cs].astype(f32)
        o_ref[rs, cs] = (u * mixed * _silu(z)).astype(o_ref.dtype)

    parts = []
    for c in range(rows // SG_CHUNK):
        parts.append(functools.partial(norm_part, c))
        parts.extend(functools.partial(mix_part, c, h) for h in range(SG_HEADS))
    return parts


def _gmlp_job(p, ln_w, ln_b, w_s, b_s, rows=512):
    t = p.shape[0]
    w = SG_WIDTH
    bias = jnp.repeat(b_s.T, LANES, axis=1)
    const2 = lambda step: (0, 0)
    return _SideJob(
        steps=t // rows,
        body=_gmlp_body,
        inputs=(p, p, p, ln_w.reshape(1, w), ln_b.reshape(1, w), w_s, bias),
        in_specs=(((rows, w), lambda step: (step, 0)),
                  ((rows, w), lambda step: (step, 1)),
                  ((rows, w), lambda step: (step, 2)),
                  ((1, w), const2),
                  ((1, w), const2),
                  (w_s.shape, lambda step: (0, 0, 0)),
                  ((SG_CHUNK, w), const2)),
        out_shapes=(jax.ShapeDtypeStruct((t, w), bf16),),
        out_specs=(((rows, w), lambda step: (step, 0)),),
        scratch=(pltpu.VMEM((rows, w), bf16),
                 pltpu.VMEM((SG_HEADS, SG_CHUNK, SG_CHUNK), bf16)))


def _rope(x, cos, sin_up, sin_dn):
    half = ROT_DIM // 2
    return (x * cos + pltpu.roll(x, LANES - half, 1) * sin_up
            + pltpu.roll(x, half, 1) * sin_dn)


def _attn_body(nb, step, sink_ref, q_ref, z_ref, kvp_ref, kvc_ref, rp_ref, rc_ref, o_ref):
    n = step % nb
    blk = ATT_BLOCK
    group = ATT_HEADS // ATT_KV_HEADS
    pre = {}

    def prepare():
        cos_c, up_c, dn_c = rc_ref[0], rc_ref[1], rc_ref[2]
        cos_p, up_p, dn_p = rp_ref[0], rp_ref[1], rp_ref[2]
        k_all = jnp.concatenate([
            _rope(kvp_ref[:, :LANES].astype(f32), cos_p, up_p, dn_p),
            _rope(kvc_ref[:, :LANES].astype(f32), cos_c, up_c, dn_c)], axis=0)
        v_all = jnp.concatenate([kvp_ref[:, LANES:].astype(f32),
                                 kvc_ref[:, LANES:].astype(f32)], axis=0)
        lane = lax.broadcasted_iota(jnp.int32, (2 * blk, LANES), 1)
        low = lane < HEAD_DIM
        k_sw = pltpu.roll(k_all, HEAD_DIM, 1)
        v_sw = pltpu.roll(v_all, HEAD_DIM, 1)
        pre["k"] = [[jnp.where(low, k_all, 0.0).astype(bf16),
                     jnp.where(low, 0.0, k_sw).astype(bf16)],
                    [jnp.where(low, k_sw, 0.0).astype(bf16),
                     jnp.where(low, 0.0, k_all).astype(bf16)]]
        pre["v"] = [[jnp.where(low, v_all, 0.0).astype(bf16),
                     jnp.where(low, 0.0, v_sw).astype(bf16)],
                    [jnp.where(low, v_sw, 0.0).astype(bf16),
                     jnp.where(low, 0.0, v_all).astype(bf16)]]
        qi = lax.broadcasted_iota(jnp.int32, (blk, blk), 0)
        kj = lax.broadcasted_iota(jnp.int32, (blk, blk), 1)
        pre["in_cur"] = kj <= qi
        pre["prev_bias"] = jnp.where(pre["in_cur"] | (n > 0), 0.0, NEG_INF)
        scale = HEAD_DIM ** -0.5
        pre["rope_q"] = (cos_c * scale, up_c * scale, dn_c * scale)

    def staged(tile_lo, tile_hi):
        heads = range(2 * tile_lo, 2 * tile_hi)
        tiles = {j: slice(j * LANES, (j + 1) * LANES) for j in range(tile_lo, tile_hi)}
        st = {}

        def scores():
            qs = {j: _rope(q_ref[:, cs].astype(f32), *pre["rope_q"]).astype(bf16)
                  for j, cs in tiles.items()}
            s2 = {hd: lax.dot_general(qs[hd // 2], pre["k"][hd // group][hd % 2],
                                      (((1,), (1,)), ((), ())), preferred_element_type=f32)
                  for hd in heads}
            st["s"] = {hd: jnp.where(pre["in_cur"], s2[hd][:, blk:], s2[hd][:, :blk])
                       + pre["prev_bias"] for hd in heads}

        def row_max():
            st["m"] = {hd: jnp.maximum(jnp.max(st["s"][hd], axis=-1, keepdims=True), sink_ref[hd])
                       for hd in heads}

        def exponent():
            st["p"] = {hd: jnp.exp(st["s"][hd] - st["m"][hd]) for hd in heads}
            st["den"] = {hd: jnp.sum(st["p"][hd], axis=-1, keepdims=True)
                         + jnp.exp(sink_ref[hd] - st["m"][hd]) for hd in heads}

        def values():
            p, in_cur = st["p"], pre["in_cur"]
            p2 = {hd: jnp.concatenate([jnp.where(in_cur, 0.0, p[hd]),
                                       jnp.where(in_cur, p[hd], 0.0)], axis=1).astype(bf16)
                  for hd in heads}
            st["pv"] = {hd: jnp.dot(p2[hd], pre["v"][hd // group][hd % 2],
                                    preferred_element_type=f32) * (1.0 / st["den"][hd])
                        for hd in heads}

        def output():
            pv = st["pv"]
            for j, cs in tiles.items():
                z = z_ref[:, cs].astype(f32)
                o_ref[:, cs] = ((pv[2 * j] + pv[2 * j + 1]) * _silu(z)).astype(o_ref.dtype)

        return [scores, row_max, exponent, values, output]

    n_tiles = ATT_HEADS // 2
    bounds = [0, 3, 6, n_tiles]
    return [prepare] + [stage for lo, hi in zip(bounds[:-1], bounds[1:])
                        for stage in staged(lo, hi)]


def _rope_tables(seq):
    half = ROT_DIM // 2
    inv_freq = ROPE_THETA ** (-jnp.arange(0, ROT_DIM, 2, dtype=f32) / ROT_DIM)
    ang = jnp.arange(seq, dtype=f32)[:, None] * inv_freq[None, :]
    cos, sin = jnp.cos(ang), jnp.sin(ang)
    ones = jnp.ones((seq, HEAD_DIM - ROT_DIM), f32)
    zeros = jnp.zeros((seq, HEAD_DIM - ROT_DIM), f32)
    zh = jnp.zeros((seq, half), f32)
    cos_t = jnp.concatenate([cos, cos, ones], axis=1)
    up_t = jnp.concatenate([-sin, zh, zeros], axis=1)
    dn_t = jnp.concatenate([zh, sin, zeros], axis=1)
    tab = jnp.stack([cos_t, up_t, dn_t])
    return jnp.concatenate([tab, tab], axis=2)


def _attn_job(p, kv, sinks, rope, bsz, seq):
    nb = seq // ATT_BLOCK
    w = ATT_WIDTH
    prev = lambda step: step - jnp.minimum(step % nb, 1)
    return _SideJob(
        steps=bsz * nb,
        body=functools.partial(_attn_body, nb),
        inputs=(sinks, p, p, kv, kv, rope, rope),
        in_specs=(None,
                  ((ATT_BLOCK, w), lambda step: (step, 3)),
                  ((ATT_BLOCK, w), lambda step: (step, 4)),
                  ((ATT_BLOCK, 2 * KV_WIDTH), lambda step: (prev(step), 0)),
                  ((ATT_BLOCK, 2 * KV_WIDTH), lambda step: (step, 0)),
                  ((3, ATT_BLOCK, LANES), lambda step: (0, prev(step) % nb, 0)),
                  ((3, ATT_BLOCK, LANES), lambda step: (0, step % nb, 0))),
        out_shapes=(jax.ShapeDtypeStruct((bsz * seq, w), bf16),),
        out_specs=(((ATT_BLOCK, w), lambda step: (step, 0)),),
        scratch=())


def _merge_kernel(ya_ref, yb_ref, yc_ref, g_ref, x_ref, wa_ref, wb_ref, wc_ref, wo_ref,
                  nw_ref, *out_refs, last, chunk):
    merged_ref = out_refs[-1]
    ys = (ya_ref, yb_ref, yc_ref)
    ws = (wa_ref, wb_ref, wc_ref)
    for c in range(D_MODEL // chunk):
        cs = slice(c * chunk, (c + 1) * chunk)
        acc = None
        for k in range(N_BRANCH):
            gate = jax.nn.sigmoid(g_ref[:, k * D_MODEL + c * chunk:k * D_MODEL + (c + 1) * chunk]
                                  .astype(f32))
            term = gate * jnp.dot(ys[k][...], ws[k][:, cs], preferred_element_type=f32)
            acc = term if acc is None else acc + term
        merged_ref[:, cs] = acc.astype(bf16)
    x_new = x_ref[...] + jnp.dot(merged_ref[...], wo_ref[...], preferred_element_type=f32)
    ms = jnp.mean(x_new * x_new, axis=-1, keepdims=True)
    normed = x_new * lax.rsqrt(ms + EPS) * nw_ref[...]
    if last:
        out_refs[0][...] = normed
    else:
        out_refs[0][...] = x_new
        out_refs[1][...] = normed.astype(bf16)


def _merge_out(ya_tm, yb, yc, gates, x, wa, wb, wc, wo, next_norm_w, bsz, seq, last, tm=256):
    t = bsz * seq
    nt = seq // tm
    d = D_MODEL
    row = lambda b, n: (b * nt + n, 0)
    const = lambda b, n: (0, 0)
    once = pl.Buffered(1)
    wspec = lambda m: pl.BlockSpec(m.shape, const, pipeline_mode=once)
    if last:
        out_shape = [jax.ShapeDtypeStruct((t, d), f32)]
        out_specs = [pl.BlockSpec((tm, d), row)]
    else:
        out_shape = [jax.ShapeDtypeStruct((t, d), f32), jax.ShapeDtypeStruct((t, d), bf16)]
        out_specs = [pl.BlockSpec((tm, d), row), pl.BlockSpec((tm, d), row)]
    return pl.pallas_call(
        functools.partial(_merge_kernel, last=last, chunk=512),
        grid=(bsz, nt),
        in_specs=[pl.BlockSpec((tm, SSM_WIDTH), lambda b, n: (n, b)),
                  pl.BlockSpec((tm, SG_WIDTH), row),
                  pl.BlockSpec((tm, ATT_WIDTH), row),
                  pl.BlockSpec((tm, N_BRANCH * d), row),
                  pl.BlockSpec((tm, d), row),
                  wspec(wa), wspec(wb), wspec(wc), wspec(wo),
                  pl.BlockSpec((1, d), const)],
        out_specs=out_specs,
        out_shape=out_shape,
        scratch_shapes=[pltpu.VMEM((tm, d), bf16)],
        compiler_params=_cparams("parallel", "parallel"),
        name="merge_out",
    )(ya_tm, yb, yc, gates, x, wa, wb, wc, wo, next_norm_w.reshape(1, d))


def kernel(x, norm_w, w_in, ssm_a_re, ssm_a_im, ssm_log_dt, ssm_b_re, ssm_b_im, ssm_c_re, ssm_c_im, ssm_d, ssm_glu_w, ssm_glu_b, sg_ln_w, sg_ln_b, sg_w, sg_b, attn_sinks, w_branch_a, w_branch_b, w_branch_c, w_out, final_norm_w):
    bsz, seq, d = x.shape
    assert 2 * bsz == SUBLANES, "the S5 kernel packs two timesteps of all batches per sublane tile"
    depth = norm_w.shape[0]
    t = bsz * seq
    xf = x.reshape(t, d)
    rope = _rope_tables(seq)

    o_ub = 2 * SSM_WIDTH
    o_q = o_ub + 3 * SG_WIDTH
    o_k = o_q + ATT_WIDTH
    o_zc = o_k + 2 * KV_WIDTH
    o_g = o_zc + ATT_WIDTH

    tn = 1024
    gate_tn = 768
    s5_cols = [0, SSM_WIDTH]
    bc_cols = [o_ub, o_ub + tn, o_ub + 2 * tn, o_q, o_zc]
    gate_cols = [o_g + c * gate_tn for c in range(N_BRANCH * d // gate_tn)]

    b4s, lam2s, c2s = jax.vmap(_s5_params)(ssm_a_re, ssm_a_im, ssm_log_dt, ssm_b_re,
                                           ssm_b_im, ssm_c_re, ssm_c_im)
    layer_weights = (ssm_glu_w, w_branch_a, w_branch_b, w_branch_c, w_out)

    h = _rmsnorm(xf, norm_w[0], bf16)
    out = None
    for l in range(depth):
        p_bc, gw, wa, wb, wc, wo = _project(
            h, w_in, l, bc_cols, tn,
            side=_cast_job(layer_weights, l, steps=len(bc_cols) * (t // 1024)), name="proj_bc")
        kv = _project(h, w_in, l, [o_k], 2 * KV_WIDTH, name="proj_kv")
        uz, yb = _project(h, w_in, l, s5_cols, tn, time_major_batches=bsz,
                          side=_gmlp_job(p_bc, sg_ln_w[l], sg_ln_b[l], sg_w[l], sg_b[l]),
                          name="proj_s5_gmlp")
        gates, yc = _project(h, w_in, l, gate_cols, gate_tn,
                             side=_attn_job(p_bc, kv, attn_sinks[l], rope, bsz, seq),
                             name="proj_gates_attn")
        ya = _s5_mixer(uz, b4s[l], lam2s[l], c2s[l], ssm_d[l], gw, ssm_glu_b[l], bsz)

        last = l == depth - 1
        next_w = final_norm_w if last else norm_w[l + 1]
        res = _merge_out(ya, yb, yc, gates, xf, wa, wb, wc, wo, next_w, bsz, seq, last)
        if last:
            out = res[0]
        else:
            xf, h = res
    return out.reshape(bsz, seq, d)
```

```python
import functools
import math
from typing import Callable, NamedTuple

import jax
import jax.numpy as jnp
from jax import lax
from jax.experimental import pallas as pl
from jax.experimental.pallas import tpu as pltpu

f32 = jnp.float32
bf16 = jnp.bfloat16

D_MODEL = 2048
EPS = 1e-6
NEG_INF = -1e30

SSM_WIDTH = D_MODEL // 2
SSM_GROUP = 16
SSM_GROUPS = SSM_WIDTH // SSM_GROUP
SSM_STATE = 64

SG_WIDTH = D_MODEL // 2
SG_HEADS = 8
SG_CHUNK = 128

HEAD_DIM = 64
ATT_HEADS = D_MODEL // 128
ATT_KV_HEADS = ATT_HEADS // 8
ATT_WIDTH = ATT_HEADS * HEAD_DIM
KV_WIDTH = ATT_KV_HEADS * HEAD_DIM
ATT_BLOCK = 128
ROT_DIM = HEAD_DIM // 4
ROPE_THETA = 500000.0
N_BRANCH = 3

LANES = 128
SUBLANES = 8
MXU_DIM = 256
VMEM_LIMIT_BYTES = 48 * 1024 * 1024

S5_GB = LANES // SSM_GROUP
S5_NGB = SSM_GROUPS // S5_GB
S5_HALF = S5_GB * SSM_STATE
S5_SW = 2 * S5_HALF


def _cparams(*sem):
    return pltpu.CompilerParams(dimension_semantics=sem, vmem_limit_bytes=VMEM_LIMIT_BYTES)


def _silu(x):
    return x * jax.nn.sigmoid(x)


def _rmsnorm_kernel(x_ref, w_ref, o_ref):
    x = x_ref[...]
    ms = jnp.mean(x * x, axis=-1, keepdims=True)
    o_ref[...] = (x * lax.rsqrt(ms + EPS) * w_ref[...]).astype(o_ref.dtype)


def _rmsnorm(x, w, out_dtype, tm=512):
    t, d = x.shape
    return pl.pallas_call(
        _rmsnorm_kernel,
        grid=(t // tm,),
        in_specs=[pl.BlockSpec((tm, d), lambda i: (i, 0)),
                  pl.BlockSpec((1, d), lambda i: (0, 0))],
        out_specs=pl.BlockSpec((tm, d), lambda i: (i, 0)),
        out_shape=jax.ShapeDtypeStruct((t, d), out_dtype),
        compiler_params=_cparams("parallel"),
        name="rmsnorm",
    )(x, w.reshape(1, d))


class _SideJob(NamedTuple):
    steps: int
    body: Callable
    inputs: tuple
    in_specs: tuple
    out_shapes: tuple
    out_specs: tuple
    scratch: tuple


_ACTIVATIONS = {None: lambda r: r, "gelu": jax.nn.gelu, "silu": _silu}


def _proj_kernel(offs_ref, h_ref, w_ref, *refs, side_body, n_side_in, n_side_out, activations):
    del offs_ref
    side_in = refs[:n_side_in]
    o_ref = refs[n_side_in]
    side_out = refs[n_side_in + 1:n_side_in + 1 + n_side_out]
    wbf_ref = refs[n_side_in + 1 + n_side_out]
    side_scratch = refs[n_side_in + 2 + n_side_out:]

    @pl.when(pl.program_id(1) == 0)
    def _():
        wbf_ref[...] = w_ref[...].astype(bf16)

    def run(name):
        act = _ACTIVATIONS[name]
        tm, tn = o_ref.shape
        if side_body is None:
            o_ref[...] = act(jnp.dot(h_ref[...], wbf_ref[...],
                                     preferred_element_type=f32)).astype(o_ref.dtype)
            return
        step = pl.program_id(0) * pl.num_programs(1) + pl.program_id(1)
        parts = side_body(step, *side_in, *side_out, *side_scratch)
        pieces = [(rs, cs) for cs in range(0, tn, MXU_DIM) for rs in range(0, tm, MXU_DIM)]
        n_parts, n_gaps = len(parts), len(pieces) - 1
        done = 0
        for k, (rs, cs) in enumerate(pieces):
            o_ref[rs:rs + MXU_DIM, cs:cs + MXU_DIM] = act(jnp.dot(
                h_ref[rs:rs + MXU_DIM, :], wbf_ref[:, cs:cs + MXU_DIM],
                preferred_element_type=f32)).astype(o_ref.dtype)
            upto = min(n_parts, -(-(k + 1) * n_parts // n_gaps))
            for part in parts[done:upto]:
                part()
            done = upto

    blocks_of = {}
    for jb, name in enumerate(activations):
        blocks_of.setdefault(name, []).append(jb)
    if len(blocks_of) == 1:
        run(activations[0])
    else:
        j = pl.program_id(0)
        for name, blocks in blocks_of.items():
            pl.when(functools.reduce(jnp.logical_or, [j == jb for jb in blocks]))(
                functools.partial(run, name))


def _project(h, w_all, layer, col_offsets, tn, tm=1024, time_major_batches=None, side=None,
             activations=None, name="proj"):
    t, k = h.shape
    nj = len(col_offsets)
    ni = t // tm
    assert all(off % LANES == 0 for off in col_offsets)
    assert side is None or side.steps == nj * ni
    offs = jnp.asarray([off // LANES for off in col_offsets], jnp.int32)
    if time_major_batches is None:
        out_shape = (t, nj * tn)
        out_map = lambda j, i, offs_ref: (i, j)
    else:
        seq = t // time_major_batches
        nt = seq // tm
        out_shape = (seq, nj * time_major_batches * tn)
        out_map = lambda j, i, offs_ref: (i % nt, j * time_major_batches + i // nt)

    def step_spec(spec):
        if spec is None:
            return pl.BlockSpec(memory_space=pltpu.SMEM)
        shape, index = spec
        return pl.BlockSpec(shape, lambda j, i, offs_ref: index(j * ni + i))

    side_in_specs = [] if side is None else [step_spec(sp) for sp in side.in_specs]
    side_out_specs = [] if side is None else [step_spec(sp) for sp in side.out_specs]
    side_inputs = () if side is None else side.inputs
    side_out_shapes = [] if side is None else list(side.out_shapes)
    side_scratch = [] if side is None else list(side.scratch)
    grid_spec = pltpu.PrefetchScalarGridSpec(
        num_scalar_prefetch=1,
        grid=(nj, ni),
        in_specs=[pl.BlockSpec((tm, k), lambda j, i, offs_ref: (i, 0)),
                  pl.BlockSpec((pl.Squeezed(), pl.Element(k), pl.Element(tn)),
                               lambda j, i, offs_ref: (layer, 0, offs_ref[j] * LANES))]
                 + side_in_specs,
        out_specs=[pl.BlockSpec((tm, tn), out_map)] + side_out_specs,
        scratch_shapes=[pltpu.VMEM((k, tn), bf16)] + side_scratch)
    res = pl.pallas_call(
        functools.partial(_proj_kernel, side_body=None if side is None else side.body,
                          n_side_in=len(side_in_specs), n_side_out=len(side_out_specs),
                          activations=tuple(activations or [None] * nj)),
        grid_spec=grid_spec,
        out_shape=[jax.ShapeDtypeStruct(out_shape, bf16)] + side_out_shapes,
        compiler_params=_cparams("arbitrary", "arbitrary"),
        name=name,
    )(offs, h, w_all, *side_inputs)
    return res[0] if side is None else res


def _tile_lanes(x, width):
    while width < LANES:
        x = x + pltpu.roll(x, width, 1)
        width *= 2
    return x


def _s5_kernel(u_ref, z_ref, b4_ref, lam2_ref, c2_ref, d_ref, gw_ref, gb_ref, o_ref,
               w2_ref, cm_ref, ubuf, bu2_ref, st2_ref, y_ref, carry_ref):
    i = pl.program_id(0)
    steps = u_ref.shape[0]
    bsz = SUBLANES // 2
    rows = steps * bsz
    w = SSM_WIDTH

    @pl.when(i == 0)
    def _():
        carry_ref[...] = jnp.zeros_like(carry_ref)
        ubuf[:, 0:SUBLANES, :] = jnp.zeros((S5_NGB, SUBLANES, LANES), f32)
        in_diag = (lax.broadcasted_iota(jnp.int32, (LANES, S5_HALF), 0) // SSM_GROUP
                   == lax.broadcasted_iota(jnp.int32, (LANES, S5_HALF), 1) // SSM_STATE)
        out_diag = (lax.broadcasted_iota(jnp.int32, (S5_HALF, LANES), 0) // SSM_STATE
                    == lax.broadcasted_iota(jnp.int32, (S5_HALF, LANES), 1) // SSM_GROUP)
        for g in range(S5_NGB):
            for k in range(4):
                x = _tile_lanes(b4_ref[k, g], SSM_STATE)
                x = jnp.concatenate([x] * (S5_HALF // LANES), axis=1)
                w2_ref[g, (k // 2) * LANES:(k // 2 + 1) * LANES,
                       (k % 2) * S5_HALF:(k % 2 + 1) * S5_HALF] = (
                           jnp.where(in_diag, x, 0.0).astype(bf16))
            for k in range(2):
                x = _tile_lanes(c2_ref[k, g], SSM_GROUP)
                cm_ref[g, k * S5_HALF:(k + 1) * S5_HALF, :] = (
                    jnp.where(out_diag, x, 0.0).astype(bf16))

    @pl.when(i > 0)
    def _():
        ubuf[:, 0:SUBLANES, :] = ubuf[:, rows:rows + SUBLANES, :]

    for g in range(S5_NGB):
        cs = slice(g * LANES, (g + 1) * LANES)
        for b in range(bsz):
            ubuf[g, pl.ds(SUBLANES + b, steps, stride=bsz), :] = (
                u_ref[:, b * w + g * LANES:b * w + (g + 1) * LANES].astype(f32))
        lhs = jnp.concatenate(
            [ubuf[g, SUBLANES:rows + SUBLANES, :].astype(bf16),
             ubuf[g, SUBLANES - bsz:rows + SUBLANES - bsz, :].astype(bf16)], axis=1)
        bu_ref = bu2_ref.at[g % 2]
        st_ref = st2_ref.at[g % 2]
        bu_ref[...] = jnp.dot(lhs, w2_ref[g], preferred_element_type=f32)
        l_re = lam2_ref[g, :, :S5_HALF]
        l_im = lam2_ref[g, :, S5_HALF:]

        def body(r, s, l_re=l_re, l_im=l_im):
            s_re, s_im = s
            off = pl.multiple_of(r * SUBLANES, SUBLANES)
            b = bu_ref[pl.ds(off, SUBLANES), :]
            n_re = l_re * s_re - l_im * s_im + b[:, :S5_HALF]
            n_im = l_re * s_im + l_im * s_re + b[:, S5_HALF:]
            st_ref[pl.ds(off, SUBLANES), :S5_HALF] = n_re
            st_ref[pl.ds(off, SUBLANES), S5_HALF:] = n_im
            return n_re, n_im

        s0 = (carry_ref[g, :, :S5_HALF], carry_ref[g, :, S5_HALF:])
        s_re, s_im = lax.fori_loop(0, rows // SUBLANES, body, s0, unroll=True)
        carry_ref[g, :, :S5_HALF] = s_re
        carry_ref[g, :, S5_HALF:] = s_im
        y_g = jnp.dot(st_ref[...].astype(bf16), cm_ref[g], preferred_element_type=f32)
        y_ref[g] = y_g + d_ref[:, cs] * ubuf[g, SUBLANES:rows + SUBLANES, :]

    y = jax.nn.gelu(jnp.concatenate([y_ref[g] for g in range(S5_NGB)], axis=1))
    gate = jnp.dot(y.astype(bf16), gw_ref[...], preferred_element_type=f32) + gb_ref[...]
    y = y * jax.nn.sigmoid(gate)
    for g in range(S5_NGB):
        y_ref[g] = y[:, g * LANES:(g + 1) * LANES]
    for b in range(bsz):
        for g in range(S5_NGB):
            cs = slice(b * w + g * LANES, b * w + (g + 1) * LANES)
            o_ref[:, cs] = (y_ref[g, pl.ds(b, steps, stride=bsz), :]
                            * z_ref[:, cs].astype(f32)).astype(o_ref.dtype)


def _s5_params(a_re, a_im, log_dt, b_re, b_im, c_re, c_im):
    dt = jnp.exp(log_dt)[:, None]
    mag = jnp.exp(a_re * dt)
    lb_re = mag * jnp.cos(a_im * dt)
    lb_im = mag * jnp.sin(a_im * dt)
    den = a_re * a_re + a_im * a_im
    k_re = ((lb_re - 1.0) * a_re + lb_im * a_im) / den
    k_im = (lb_im * a_re - (lb_re - 1.0) * a_im) / den
    bb_re = k_re[..., None] * b_re - k_im[..., None] * b_im
    bb_im = k_re[..., None] * b_im + k_im[..., None] * b_re
    t_re = lb_re[..., None] * bb_re - lb_im[..., None] * bb_im
    t_im = lb_re[..., None] * bb_im + lb_im[..., None] * bb_re

    def pad_lanes(m):
        return jnp.pad(m, ((0, 0), (0, 0), (0, LANES - m.shape[-1])))

    b4 = jnp.stack([pad_lanes(m.transpose(0, 2, 1).reshape(S5_NGB, LANES, SSM_STATE))
                    for m in (bb_re, bb_im, t_re, t_im)])
    c2 = jnp.stack([pad_lanes(m.transpose(0, 2, 1).reshape(S5_NGB, S5_HALF, SSM_GROUP))
                    for m in (c_re, -c_im)])
    l2_re = (lb_re * lb_re - lb_im * lb_im).reshape(S5_NGB, 1, S5_HALF)
    l2_im = (2.0 * lb_re * lb_im).reshape(S5_NGB, 1, S5_HALF)
    lam2 = jnp.broadcast_to(jnp.concatenate([l2_re, l2_im], axis=2),
                            (S5_NGB, SUBLANES, S5_SW))
    return b4, lam2, c2


def _s5_mixer(uz, b4, lam2, c2, d, glu_w, glu_b, bsz, steps=128):
    seq = uz.shape[0]
    w = SSM_WIDTH
    bw = bsz * w
    rows = steps * bsz
    const4 = lambda i: (0, 0, 0, 0)
    const3 = lambda i: (0, 0, 0)
    const2 = lambda i: (0, 0)
    return pl.pallas_call(
        _s5_kernel,
        grid=(seq // steps,),
        in_specs=[pl.BlockSpec((steps, bw), lambda i: (i, 0)),
                  pl.BlockSpec((steps, bw), lambda i: (i, 1)),
                  pl.BlockSpec(b4.shape, const4),
                  pl.BlockSpec(lam2.shape, const3),
                  pl.BlockSpec(c2.shape, const4),
                  pl.BlockSpec((1, w), const2),
                  pl.BlockSpec((w, w), const2),
                  pl.BlockSpec((1, w), const2)],
        out_specs=pl.BlockSpec((steps, bw), lambda i: (i, 0)),
        out_shape=jax.ShapeDtypeStruct((seq, bw), bf16),
        scratch_shapes=[pltpu.VMEM((S5_NGB, 2 * LANES, S5_SW), bf16),
                        pltpu.VMEM((S5_NGB, S5_SW, LANES), bf16),
                        pltpu.VMEM((S5_NGB, rows + SUBLANES, LANES), f32),
                        pltpu.VMEM((2, rows, S5_SW), f32),
                        pltpu.VMEM((2, rows, S5_SW), f32),
                        pltpu.VMEM((S5_NGB, rows, LANES), f32),
                        pltpu.VMEM((S5_NGB, SUBLANES, S5_SW), f32)],
        compiler_params=_cparams("arbitrary"),
        name="s5_mixer",
    )(uz, uz, b4, lam2, c2, d.reshape(1, w), glu_w, glu_b.reshape(1, w))


def _cast_body(n_arrays, step, *refs):
    del step

    def cast():
        for src, dst in zip(refs[:n_arrays], refs[n_arrays:]):
            dst[...] = src[...].astype(dst.dtype)

    return [cast]


def _cast_job(arrays, layer, steps, n_blocks=32):
    block = lambda step: (jnp.minimum(step, n_blocks - 1), 0)
    specs, shapes = [], []
    for m in arrays:
        rows = m.shape[1] // n_blocks
        assert rows * n_blocks == m.shape[1] and rows % (2 * SUBLANES) == 0
        specs.append(((pl.Squeezed(), rows, m.shape[2]),
                      lambda step: (layer, jnp.minimum(step, n_blocks - 1), 0)))
        shapes.append(((rows, m.shape[2]), jax.ShapeDtypeStruct(m.shape[1:], bf16)))
    return _SideJob(
        steps=steps,
        body=functools.partial(_cast_body, len(arrays)),
        inputs=tuple(arrays),
        in_specs=tuple(specs),
        out_shapes=tuple(sh for _, sh in shapes),
        out_specs=tuple((blk, block) for blk, _ in shapes),
        scratch=())


def _gmlp_body(step, u_ref, v_ref, z_ref, lnw_ref, lnb_ref, ws_ref, bias_ref, o_ref,
               vn_ref, wm_ref):
    del step
    rows = u_ref.shape[0]

    def norm_part(c):
        rs = slice(c * SG_CHUNK, (c + 1) * SG_CHUNK)
        if c == 0:
            t_idx = lax.broadcasted_iota(jnp.int32, (SG_CHUNK, SG_CHUNK), 0)
            s_idx = lax.broadcasted_iota(jnp.int32, (SG_CHUNK, SG_CHUNK), 1)
            for h in range(SG_HEADS):
                wm_ref[h] = jnp.where(s_idx <= t_idx, ws_ref[h], 0.0).astype(bf16)
        v = v_ref[rs, :].astype(f32)
        mu = jnp.mean(v, axis=-1, keepdims=True)
        vc = v - mu
        var = jnp.mean(vc * vc, axis=-1, keepdims=True)
        vn_ref[rs, :] = (vc * lax.rsqrt(var + EPS) * lnw_ref[...] + lnb_ref[...]).astype(bf16)

    def mix_part(c, h):
        rs = slice(c * SG_CHUNK, (c + 1) * SG_CHUNK)
        cs = slice(h * LANES, (h + 1) * LANES)
        mixed = jnp.dot(wm_ref[h], vn_ref[rs, cs], preferred_element_type=f32)
        mixed = mixed + bias_ref[:, cs]
        o_ref[rs, cs] = (u_ref[rs, cs].astype(f32) * mixed
                         * z_ref[rs, cs].astype(f32)).astype(o_ref.dtype)

    parts = []
    for c in range(rows // SG_CHUNK):
        parts.append(functools.partial(norm_part, c))
        parts.extend(functools.partial(mix_part, c, h) for h in range(SG_HEADS))
    return parts


def _gmlp_job(p, ln_w, ln_b, w_s, b_s, rows=512):
    t = p.shape[0]
    w = SG_WIDTH
    bias = jnp.repeat(b_s.T, LANES, axis=1)
    const2 = lambda step: (0, 0)
    return _SideJob(
        steps=t // rows,
        body=_gmlp_body,
        inputs=(p, p, p, ln_w.reshape(1, w), ln_b.reshape(1, w), w_s, bias),
        in_specs=(((rows, w), lambda step: (step, 0)),
                  ((rows, w), lambda step: (step, 1)),
                  ((rows, w), lambda step: (step, 2)),
                  ((1, w), const2),
                  ((1, w), const2),
                  (w_s.shape, lambda step: (0, 0, 0)),
                  ((SG_CHUNK, w), const2)),
        out_shapes=(jax.ShapeDtypeStruct((t, w), bf16),),
        out_specs=(((rows, w), lambda step: (step, 0)),),
        scratch=(pltpu.VMEM((rows, w), bf16),
                 pltpu.VMEM((SG_HEADS, SG_CHUNK, SG_CHUNK), bf16)))


def _rope(x, cos, sin_up, sin_dn):
    half = ROT_DIM // 2
    return (x * cos + pltpu.roll(x, LANES - half, 1) * sin_up
            + pltpu.roll(x, half, 1) * sin_dn)


def _attn_body(nb, step, sink_ref, q_ref, z_ref, kvp_ref, kvc_ref, rp_ref, rc_ref, o_ref):
    n = step % nb
    blk = ATT_BLOCK
    group = ATT_HEADS // ATT_KV_HEADS
    pre = {}

    def prepare():
        cos_c, up_c, dn_c = rc_ref[0], rc_ref[1], rc_ref[2]
        cos_p, up_p, dn_p = rp_ref[0], rp_ref[1], rp_ref[2]
        k_all = jnp.concatenate([
            _rope(kvp_ref[:, :LANES].astype(f32), cos_p, up_p, dn_p),
            _rope(kvc_ref[:, :LANES].astype(f32), cos_c, up_c, dn_c)], axis=0)
        v_all = jnp.concatenate([kvp_ref[:, LANES:].astype(f32),
                                 kvc_ref[:, LANES:].astype(f32)], axis=0)
        lane = lax.broadcasted_iota(jnp.int32, (2 * blk, LANES), 1)
        low = lane < HEAD_DIM
        k_sw = pltpu.roll(k_all, HEAD_DIM, 1)
        v_sw = pltpu.roll(v_all, HEAD_DIM, 1)
        pre["k"] = [[jnp.where(low, k_all, 0.0).astype(bf16),
                     jnp.where(low, 0.0, k_sw).astype(bf16)],
                    [jnp.where(low, k_sw, 0.0).astype(bf16),
                     jnp.where(low, 0.0, k_all).astype(bf16)]]
        pre["v"] = [[jnp.where(low, v_all, 0.0).astype(bf16),
                     jnp.where(low, 0.0, v_sw).astype(bf16)],
                    [jnp.where(low, v_sw, 0.0).astype(bf16),
                     jnp.where(low, 0.0, v_all).astype(bf16)]]
        qi = lax.broadcasted_iota(jnp.int32, (blk, blk), 0)
        kj = lax.broadcasted_iota(jnp.int32, (blk, blk), 1)
        pre["in_cur"] = kj <= qi
        pre["prev_bias"] = jnp.where(pre["in_cur"] | (n > 0), 0.0, NEG_INF)
        scale = HEAD_DIM ** -0.5
        pre["rope_q"] = (cos_c * scale, up_c * scale, dn_c * scale)

    def staged(tile_lo, tile_hi):
        heads = range(2 * tile_lo, 2 * tile_hi)
        tiles = {j: slice(j * LANES, (j + 1) * LANES) for j in range(tile_lo, tile_hi)}
        st = {}

        def scores():
            qs = {j: _rope(q_ref[:, cs].astype(f32), *pre["rope_q"]).astype(bf16)
                  for j, cs in tiles.items()}
            s2 = {hd: lax.dot_general(qs[hd // 2], pre["k"][hd // group][hd % 2],
                                      (((1,), (1,)), ((), ())), preferred_element_type=f32)
                  for hd in heads}
            st["s"] = {hd: jnp.where(pre["in_cur"], s2[hd][:, blk:], s2[hd][:, :blk])
                       + pre["prev_bias"] for hd in heads}

        def row_max():
            st["m"] = {hd: jnp.maximum(jnp.max(st["s"][hd], axis=-1, keepdims=True), sink_ref[hd])
                       for hd in heads}

        def exponent():
            st["p"] = {hd: jnp.exp(st["s"][hd] - st["m"][hd]) for hd in heads}
            st["den"] = {hd: jnp.sum(st["p"][hd], axis=-1, keepdims=True)
                         + jnp.exp(sink_ref[hd] - st["m"][hd]) for hd in heads}

        def values():
            p, in_cur = st["p"], pre["in_cur"]
            p2 = {hd: jnp.concatenate([jnp.where(in_cur, 0.0, p[hd]),
                                       jnp.where(in_cur, p[hd], 0.0)], axis=1).astype(bf16)
                  for hd in heads}
            st["pv"] = {hd: jnp.dot(p2[hd], pre["v"][hd // group][hd % 2],
                                    preferred_element_type=f32) * (1.0 / st["den"][hd])
                        for hd in heads}

        def output():
            pv = st["pv"]
            for j, cs in tiles.items():
                o_ref[:, cs] = ((pv[2 * j] + pv[2 * j + 1])
                                * z_ref[:, cs].astype(f32)).astype(o_ref.dtype)

        return [scores, row_max, exponent, values, output]

    n_tiles = ATT_HEADS // 2
    bounds = [0, 3, 6, n_tiles]
    return [prepare] + [stage for lo, hi in zip(bounds[:-1], bounds[1:])
                        for stage in staged(lo, hi)]


def _rope_tables(seq):
    half = ROT_DIM // 2
    inv_freq = ROPE_THETA ** (-jnp.arange(0, ROT_DIM, 2, dtype=f32) / ROT_DIM)
    ang = jnp.arange(seq, dtype=f32)[:, None] * inv_freq[None, :]
    cos, sin = jnp.cos(ang), jnp.sin(ang)
    ones = jnp.ones((seq, HEAD_DIM - ROT_DIM), f32)
    zeros = jnp.zeros((seq, HEAD_DIM - ROT_DIM), f32)
    zh = jnp.zeros((seq, half), f32)
    cos_t = jnp.concatenate([cos, cos, ones], axis=1)
    up_t = jnp.concatenate([-sin, zh, zeros], axis=1)
    dn_t = jnp.concatenate([zh, sin, zeros], axis=1)
    tab = jnp.stack([cos_t, up_t, dn_t])
    return jnp.concatenate([tab, tab], axis=2)


def _attn_job(p, kv, sinks, rope, bsz, seq):
    nb = seq // ATT_BLOCK
    w = ATT_WIDTH
    prev = lambda step: step - jnp.minimum(step % nb, 1)
    return _SideJob(
        steps=bsz * nb,
        body=functools.partial(_attn_body, nb),
        inputs=(sinks, p, p, kv, kv, rope, rope),
        in_specs=(None,
                  ((ATT_BLOCK, w), lambda step: (step, 3)),
                  ((ATT_BLOCK, w), lambda step: (step, 4)),
                  ((ATT_BLOCK, 2 * KV_WIDTH), lambda step: (prev(step), 0)),
                  ((ATT_BLOCK, 2 * KV_WIDTH), lambda step: (step, 0)),
                  ((3, ATT_BLOCK, LANES), lambda step: (0, prev(step) % nb, 0)),
                  ((3, ATT_BLOCK, LANES), lambda step: (0, step % nb, 0))),
        out_shapes=(jax.ShapeDtypeStruct((bsz * seq, w), bf16),),
        out_specs=(((ATT_BLOCK, w), lambda step: (step, 0)),),
        scratch=())


def _merge_kernel(ya_ref, yb_ref, yc_ref, g_ref, x_ref, wa_ref, wb_ref, wc_ref, wo_ref,
                  nw_ref, *out_refs, last, chunk):
    merged_ref = out_refs[-1]
    ys = (ya_ref, yb_ref, yc_ref)
    ws = (wa_ref, wb_ref, wc_ref)
    for c in range(D_MODEL // chunk):
        cs = slice(c * chunk, (c + 1) * chunk)
        acc = None
        for k in range(N_BRANCH):
            gate = jax.nn.sigmoid(g_ref[:, k * D_MODEL + c * chunk:k * D_MODEL + (c + 1) * chunk]
                                  .astype(f32))
            term = gate * jnp.dot(ys[k][...], ws[k][:, cs], preferred_element_type=f32)
            acc = term if acc is None else acc + term
        merged_ref[:, cs] = acc.astype(bf16)
    x_new = x_ref[...] + jnp.dot(merged_ref[...], wo_ref[...], preferred_element_type=f32)
    ms = jnp.mean(x_new * x_new, axis=-1, keepdims=True)
    normed = x_new * lax.rsqrt(ms + EPS) * nw_ref[...]
    if last:
        out_refs[0][...] = normed
    else:
        out_refs[0][...] = x_new
        out_refs[1][...] = normed.astype(bf16)


def _merge_out(ya_tm, yb, yc, gates, x, wa, wb, wc, wo, next_norm_w, bsz, seq, last, tm=256):
    t = bsz * seq
    nt = seq // tm
    d = D_MODEL
    row = lambda b, n: (b * nt + n, 0)
    const = lambda b, n: (0, 0)
    once = pl.Buffered(1)
    wspec = lambda m: pl.BlockSpec(m.shape, const, pipeline_mode=once)
    if last:
        out_shape = [jax.ShapeDtypeStruct((t, d), f32)]
        out_specs = [pl.BlockSpec((tm, d), row)]
    else:
        out_shape = [jax.ShapeDtypeStruct((t, d), f32), jax.ShapeDtypeStruct((t, d), bf16)]
        out_specs = [pl.BlockSpec((tm, d), row), pl.BlockSpec((tm, d), row)]
    return pl.pallas_call(
        functools.partial(_merge_kernel, last=last, chunk=512),
        grid=(bsz, nt),
        in_specs=[pl.BlockSpec((tm, SSM_WIDTH), lambda b, n: (n, b)),
                  pl.BlockSpec((tm, SG_WIDTH), row),
                  pl.BlockSpec((tm, ATT_WIDTH), row),
                  pl.BlockSpec((tm, N_BRANCH * d), row),
                  pl.BlockSpec((tm, d), row),
                  wspec(wa), wspec(wb), wspec(wc), wspec(wo),
                  pl.BlockSpec((1, d), const)],
        out_specs=out_specs,
        out_shape=out_shape,
        scratch_shapes=[pltpu.VMEM((tm, d), bf16)],
        compiler_params=_cparams("parallel", "parallel"),
        name="merge_out",
    )(ya_tm, yb, yc, gates, x, wa, wb, wc, wo, next_norm_w.reshape(1, d))


def kernel(x, norm_w, w_in, ssm_a_re, ssm_a_im, ssm_log_dt, ssm_b_re, ssm_b_im, ssm_c_re, ssm_c_im, ssm_d, ssm_glu_w, ssm_glu_b, sg_ln_w, sg_ln_b, sg_w, sg_b, attn_sinks, w_branch_a, w_branch_b, w_branch_c, w_out, final_norm_w):
    bsz, seq, d = x.shape
    assert 2 * bsz == SUBLANES, "the S5 kernel packs two timesteps of all batches per sublane tile"
    depth = norm_w.shape[0]
    t = bsz * seq
    xf = x.reshape(t, d)
    rope = _rope_tables(seq)

    o_ub = 2 * SSM_WIDTH
    o_q = o_ub + 3 * SG_WIDTH
    o_k = o_q + ATT_WIDTH
    o_zc = o_k + 2 * KV_WIDTH
    o_g = o_zc + ATT_WIDTH

    tn = 1024
    gate_tn = 768
    s5_cols = [0, SSM_WIDTH]
    bc_cols = [o_ub, o_ub + tn, o_ub + 2 * tn, o_q, o_zc]
    gate_cols = [o_g + c * gate_tn for c in range(N_BRANCH * d // gate_tn)]

    b4s, lam2s, c2s = jax.vmap(_s5_params)(ssm_a_re, ssm_a_im, ssm_log_dt, ssm_b_re,
                                           ssm_b_im, ssm_c_re, ssm_c_im)
    layer_weights = (ssm_glu_w, w_branch_a, w_branch_b, w_branch_c, w_out)

    h = _rmsnorm(xf, norm_w[0], bf16)
    out = None
    for l in range(depth):
        p_bc, gw, wa, wb, wc, wo = _project(
            h, w_in, l, bc_cols, tn,
            side=_cast_job(layer_weights, l, steps=len(bc_cols) * (t // 1024)),
            activations=["gelu", "gelu", "silu", None, "silu"], name="proj_bc")
        kv = _project(h, w_in, l, [o_k], 2 * KV_WIDTH, name="proj_kv")
        uz, yb = _project(h, w_in, l, s5_cols, tn, time_major_batches=bsz,
                          side=_gmlp_job(p_bc, sg_ln_w[l], sg_ln_b[l], sg_w[l], sg_b[l]),
                          activations=[None, "silu"], name="proj_s5_gmlp")
        gates, yc = _project(h, w_in, l, gate_cols, gate_tn,
                             side=_attn_job(p_bc, kv, attn_sinks[l], rope, bsz, seq),
                             name="proj_gates_attn")
        ya = _s5_mixer(uz, b4s[l], lam2s[l], c2s[l], ssm_d[l], gw, ssm_glu_b[l], bsz)

        last = l == depth - 1
        next_w = final_norm_w if last else norm_w[l + 1]
        res = _merge_out(ya, yb, yc, gates, xf, wa, wb, wc, wo, next_w, bsz, seq, last)
        if last:
            out = res[0]
        else:
            xf, h = res
    return out.reshape(bsz, seq, d)
```

```python
import functools
import math
from typing import Callable, NamedTuple

import jax
import jax.numpy as jnp
from jax import lax
from jax.experimental import pallas as pl
from jax.experimental.pallas import tpu as pltpu

f32 = jnp.float32
bf16 = jnp.bfloat16

D_MODEL = 2048
EPS = 1e-6
NEG_INF = -1e30

SSM_WIDTH = D_MODEL // 2
SSM_GROUP = 16
SSM_GROUPS = SSM_WIDTH // SSM_GROUP
SSM_STATE = 64

SG_WIDTH = D_MODEL // 2
SG_HEADS = 8
SG_CHUNK = 128

HEAD_DIM = 64
ATT_HEADS = D_MODEL // 128
ATT_KV_HEADS = ATT_HEADS // 8
ATT_WIDTH = ATT_HEADS * HEAD_DIM
KV_WIDTH = ATT_KV_HEADS * HEAD_DIM
ATT_BLOCK = 128
ROT_DIM = HEAD_DIM // 4
ROPE_THETA = 500000.0
N_BRANCH = 3

LANES = 128
SUBLANES = 8
MXU_DIM = 256
VMEM_LIMIT_BYTES = 56 * 1024 * 1024

S5_GB = LANES // SSM_GROUP
S5_NGB = SSM_GROUPS // S5_GB
S5_HALF = S5_GB * SSM_STATE
S5_SW = 2 * S5_HALF


def _cparams(*sem):
    return pltpu.CompilerParams(dimension_semantics=sem, vmem_limit_bytes=VMEM_LIMIT_BYTES)


def _silu(x):
    return x * jax.nn.sigmoid(x)


def _rmsnorm_kernel(x_ref, w_ref, o_ref):
    x = x_ref[...]
    ms = jnp.mean(x * x, axis=-1, keepdims=True)
    o_ref[...] = (x * lax.rsqrt(ms + EPS) * w_ref[...]).astype(o_ref.dtype)


def _rmsnorm(x, w, out_dtype, tm=512):
    t, d = x.shape
    return pl.pallas_call(
        _rmsnorm_kernel,
        grid=(t // tm,),
        in_specs=[pl.BlockSpec((tm, d), lambda i: (i, 0)),
                  pl.BlockSpec((1, d), lambda i: (0, 0))],
        out_specs=pl.BlockSpec((tm, d), lambda i: (i, 0)),
        out_shape=jax.ShapeDtypeStruct((t, d), out_dtype),
        compiler_params=_cparams("parallel"),
        name="rmsnorm",
    )(x, w.reshape(1, d))


class _SideJob(NamedTuple):
    steps: int
    body: Callable
    inputs: tuple
    in_specs: tuple
    out_shapes: tuple
    out_specs: tuple
    scratch: tuple


_ACTIVATIONS = {None: lambda r: r, "gelu": jax.nn.gelu, "silu": _silu}


def _proj_kernel(offs_ref, h_ref, w_ref, *refs, side_body, n_side_in, n_side_out, activations):
    del offs_ref
    side_in = refs[:n_side_in]
    o_ref = refs[n_side_in]
    side_out = refs[n_side_in + 1:n_side_in + 1 + n_side_out]
    wbf_ref = refs[n_side_in + 1 + n_side_out]
    side_scratch = refs[n_side_in + 2 + n_side_out:]

    @pl.when(pl.program_id(1) == 0)
    def _():
        wbf_ref[...] = w_ref[...].astype(bf16)

    def run(name):
        act = _ACTIVATIONS[name]
        tm, tn = o_ref.shape
        if side_body is None:
            o_ref[...] = act(jnp.dot(h_ref[...], wbf_ref[...],
                                     preferred_element_type=f32)).astype(o_ref.dtype)
            return
        step = pl.program_id(0) * pl.num_programs(1) + pl.program_id(1)
        parts = side_body(step, *side_in, *side_out, *side_scratch)
        pieces = [(rs, cs) for cs in range(0, tn, MXU_DIM) for rs in range(0, tm, MXU_DIM)]
        n_parts, n_gaps = len(parts), len(pieces) - 1
        done = 0
        for k, (rs, cs) in enumerate(pieces):
            o_ref[rs:rs + MXU_DIM, cs:cs + MXU_DIM] = act(jnp.dot(
                h_ref[rs:rs + MXU_DIM, :], wbf_ref[:, cs:cs + MXU_DIM],
                preferred_element_type=f32)).astype(o_ref.dtype)
            upto = min(n_parts, -(-(k + 1) * n_parts // n_gaps))
            for part in parts[done:upto]:
                part()
            done = upto

    blocks_of = {}
    for jb, name in enumerate(activations):
        blocks_of.setdefault(name, []).append(jb)
    if len(blocks_of) == 1:
        run(activations[0])
    else:
        j = pl.program_id(0)
        for name, blocks in blocks_of.items():
            pl.when(functools.reduce(jnp.logical_or, [j == jb for jb in blocks]))(
                functools.partial(run, name))


def _project(h, w_all, layer, col_offsets, tn, tm=1024, time_major_batches=None, side=None,
             activations=None, name="proj"):
    t, k = h.shape
    nj = len(col_offsets)
    ni = t // tm
    assert all(off % LANES == 0 for off in col_offsets)
    assert side is None or side.steps == nj * ni
    offs = jnp.asarray([off // LANES for off in col_offsets], jnp.int32)
    if time_major_batches is None:
        out_shape = (t, nj * tn)
        out_map = lambda j, i, offs_ref: (i, j)
    else:
        seq = t // time_major_batches
        nt = seq // tm
        out_shape = (seq, nj * time_major_batches * tn)
        out_map = lambda j, i, offs_ref: (i % nt, j * time_major_batches + i // nt)

    def step_spec(spec):
        if spec is None:
            return pl.BlockSpec(memory_space=pltpu.SMEM)
        shape, index = spec
        return pl.BlockSpec(shape, lambda j, i, offs_ref: index(j * ni + i))

    side_in_specs = [] if side is None else [step_spec(sp) for sp in side.in_specs]
    side_out_specs = [] if side is None else [step_spec(sp) for sp in side.out_specs]
    side_inputs = () if side is None else side.inputs
    side_out_shapes = [] if side is None else list(side.out_shapes)
    side_scratch = [] if side is None else list(side.scratch)
    grid_spec = pltpu.PrefetchScalarGridSpec(
        num_scalar_prefetch=1,
        grid=(nj, ni),
        in_specs=[pl.BlockSpec((tm, k), lambda j, i, offs_ref: (i, 0)),
                  pl.BlockSpec((pl.Squeezed(), pl.Element(k), pl.Element(tn)),
                               lambda j, i, offs_ref: (layer, 0, offs_ref[j] * LANES))]
                 + side_in_specs,
        out_specs=[pl.BlockSpec((tm, tn), out_map)] + side_out_specs,
        scratch_shapes=[pltpu.VMEM((k, tn), bf16)] + side_scratch)
    res = pl.pallas_call(
        functools.partial(_proj_kernel, side_body=None if side is None else side.body,
                          n_side_in=len(side_in_specs), n_side_out=len(side_out_specs),
                          activations=tuple(activations or [None] * nj)),
        grid_spec=grid_spec,
        out_shape=[jax.ShapeDtypeStruct(out_shape, bf16)] + side_out_shapes,
        compiler_params=_cparams("arbitrary", "arbitrary"),
        name=name,
    )(offs, h, w_all, *side_inputs)
    return res[0] if side is None else res


def _tile_lanes(x, width):
    while width < LANES:
        x = x + pltpu.roll(x, width, 1)
        width *= 2
    return x


def _s5_kernel(u_ref, z_ref, b4_ref, lam2_ref, c2_ref, d_ref, gw_ref, gb_ref, o_ref,
               w2_ref, cm_ref, ubuf, bu2_ref, st2_ref, y_ref, carry_ref):
    i = pl.program_id(0)
    steps = u_ref.shape[0]
    bsz = SUBLANES // 2
    rows = steps * bsz
    w = SSM_WIDTH

    @pl.when(i == 0)
    def _():
        carry_ref[...] = jnp.zeros_like(carry_ref)
        ubuf[:, 0:SUBLANES, :] = jnp.zeros((S5_NGB, SUBLANES, LANES), f32)
        in_diag = (lax.broadcasted_iota(jnp.int32, (LANES, S5_HALF), 0) // SSM_GROUP
                   == lax.broadcasted_iota(jnp.int32, (LANES, S5_HALF), 1) // SSM_STATE)
        out_diag = (lax.broadcasted_iota(jnp.int32, (S5_HALF, LANES), 0) // SSM_STATE
                    == lax.broadcasted_iota(jnp.int32, (S5_HALF, LANES), 1) // SSM_GROUP)
        for g in range(S5_NGB):
            for k in range(4):
                x = _tile_lanes(b4_ref[k, g], SSM_STATE)
                x = jnp.concatenate([x] * (S5_HALF // LANES), axis=1)
                w2_ref[g, (k // 2) * LANES:(k // 2 + 1) * LANES,
                       (k % 2) * S5_HALF:(k % 2 + 1) * S5_HALF] = (
                           jnp.where(in_diag, x, 0.0).astype(bf16))
            for k in range(2):
                x = _tile_lanes(c2_ref[k, g], SSM_GROUP)
                cm_ref[g, k * S5_HALF:(k + 1) * S5_HALF, :] = (
                    jnp.where(out_diag, x, 0.0).astype(bf16))

    @pl.when(i > 0)
    def _():
        ubuf[:, 0:SUBLANES, :] = ubuf[:, rows:rows + SUBLANES, :]

    for g in range(S5_NGB):
        cs = slice(g * LANES, (g + 1) * LANES)
        for b in range(bsz):
            ubuf[g, pl.ds(SUBLANES + b, steps, stride=bsz), :] = (
                u_ref[:, b * w + g * LANES:b * w + (g + 1) * LANES].astype(f32))
        lhs = jnp.concatenate(
            [ubuf[g, SUBLANES:rows + SUBLANES, :].astype(bf16),
             ubuf[g, SUBLANES - bsz:rows + SUBLANES - bsz, :].astype(bf16)], axis=1)
        bu_ref = bu2_ref.at[g % 2]
        st_ref = st2_ref.at[g % 2]
        bu_ref[...] = jnp.dot(lhs, w2_ref[g], preferred_element_type=f32)
        l_re = lam2_ref[g, :, :S5_HALF]
        l_im = lam2_ref[g, :, S5_HALF:]

        def body(r, s, l_re=l_re, l_im=l_im):
            s_re, s_im = s
            off = pl.multiple_of(r * SUBLANES, SUBLANES)
            b = bu_ref[pl.ds(off, SUBLANES), :]
            n_re = l_re * s_re - l_im * s_im + b[:, :S5_HALF]
            n_im = l_re * s_im + l_im * s_re + b[:, S5_HALF:]
            st_ref[pl.ds(off, SUBLANES), :S5_HALF] = n_re
            st_ref[pl.ds(off, SUBLANES), S5_HALF:] = n_im
            return n_re, n_im

        s0 = (carry_ref[g, :, :S5_HALF], carry_ref[g, :, S5_HALF:])
        s_re, s_im = lax.fori_loop(0, rows // SUBLANES, body, s0, unroll=True)
        carry_ref[g, :, :S5_HALF] = s_re
        carry_ref[g, :, S5_HALF:] = s_im
        y_g = jnp.dot(st_ref[...].astype(bf16), cm_ref[g], preferred_element_type=f32)
        y_ref[g] = y_g + d_ref[:, cs] * ubuf[g, SUBLANES:rows + SUBLANES, :]

    y = jax.nn.gelu(jnp.concatenate([y_ref[g] for g in range(S5_NGB)], axis=1))
    gate = jnp.dot(y.astype(bf16), gw_ref[...], preferred_element_type=f32) + gb_ref[...]
    y = y * jax.nn.sigmoid(gate)
    for g in range(S5_NGB):
        y_ref[g] = y[:, g * LANES:(g + 1) * LANES]
    for b in range(bsz):
        for g in range(S5_NGB):
            cs = slice(b * w + g * LANES, b * w + (g + 1) * LANES)
            o_ref[:, cs] = (y_ref[g, pl.ds(b, steps, stride=bsz), :]
                            * z_ref[:, cs].astype(f32)).astype(o_ref.dtype)


def _s5_params(a_re, a_im, log_dt, b_re, b_im, c_re, c_im):
    dt = jnp.exp(log_dt)[:, None]
    mag = jnp.exp(a_re * dt)
    lb_re = mag * jnp.cos(a_im * dt)
    lb_im = mag * jnp.sin(a_im * dt)
    den = a_re * a_re + a_im * a_im
    k_re = ((lb_re - 1.0) * a_re + lb_im * a_im) / den
    k_im = (lb_im * a_re - (lb_re - 1.0) * a_im) / den
    bb_re = k_re[..., None] * b_re - k_im[..., None] * b_im
    bb_im = k_re[..., None] * b_im + k_im[..., None] * b_re
    t_re = lb_re[..., None] * bb_re - lb_im[..., None] * bb_im
    t_im = lb_re[..., None] * bb_im + lb_im[..., None] * bb_re

    def pad_lanes(m):
        return jnp.pad(m, ((0, 0), (0, 0), (0, LANES - m.shape[-1])))

    b4 = jnp.stack([pad_lanes(m.transpose(0, 2, 1).reshape(S5_NGB, LANES, SSM_STATE))
                    for m in (bb_re, bb_im, t_re, t_im)])
    c2 = jnp.stack([pad_lanes(m.transpose(0, 2, 1).reshape(S5_NGB, S5_HALF, SSM_GROUP))
                    for m in (c_re, -c_im)])
    l2_re = (lb_re * lb_re - lb_im * lb_im).reshape(S5_NGB, 1, S5_HALF)
    l2_im = (2.0 * lb_re * lb_im).reshape(S5_NGB, 1, S5_HALF)
    lam2 = jnp.broadcast_to(jnp.concatenate([l2_re, l2_im], axis=2),
                            (S5_NGB, SUBLANES, S5_SW))
    return b4, lam2, c2


def _s5_mixer(uz, b4, lam2, c2, d, glu_w, glu_b, bsz, steps=128):
    seq = uz.shape[0]
    w = SSM_WIDTH
    bw = bsz * w
    rows = steps * bsz
    const4 = lambda i: (0, 0, 0, 0)
    const3 = lambda i: (0, 0, 0)
    const2 = lambda i: (0, 0)
    return pl.pallas_call(
        _s5_kernel,
        grid=(seq // steps,),
        in_specs=[pl.BlockSpec((steps, bw), lambda i: (i, 0)),
                  pl.BlockSpec((steps, bw), lambda i: (i, 1)),
                  pl.BlockSpec(b4.shape, const4),
                  pl.BlockSpec(lam2.shape, const3),
                  pl.BlockSpec(c2.shape, const4),
                  pl.BlockSpec((1, w), const2),
                  pl.BlockSpec((w, w), const2),
                  pl.BlockSpec((1, w), const2)],
        out_specs=pl.BlockSpec((steps, bw), lambda i: (i, 0)),
        out_shape=jax.ShapeDtypeStruct((seq, bw), bf16),
        scratch_shapes=[pltpu.VMEM((S5_NGB, 2 * LANES, S5_SW), bf16),
                        pltpu.VMEM((S5_NGB, S5_SW, LANES), bf16),
                        pltpu.VMEM((S5_NGB, rows + SUBLANES, LANES), f32),
                        pltpu.VMEM((2, rows, S5_SW), f32),
                        pltpu.VMEM((2, rows, S5_SW), f32),
                        pltpu.VMEM((S5_NGB, rows, LANES), f32),
                        pltpu.VMEM((S5_NGB, SUBLANES, S5_SW), f32)],
        compiler_params=_cparams("arbitrary"),
        name="s5_mixer",
    )(uz, uz, b4, lam2, c2, d.reshape(1, w), glu_w, glu_b.reshape(1, w))


def _cast_body(n_arrays, every, step, *refs):
    def cast():
        @pl.when(step % every == 0)
        def _():
            for src, dst in zip(refs[:n_arrays], refs[n_arrays:]):
                dst[...] = src[...].astype(dst.dtype)

    return [cast]


def _cast_job(arrays, layer, steps, every):
    assert steps % every == 0
    n_blocks = steps // every
    specs, shapes = [], []
    for m in arrays:
        rows = m.shape[1] // n_blocks
        assert rows * n_blocks == m.shape[1] and rows % (2 * SUBLANES) == 0
        specs.append(((pl.Squeezed(), rows, m.shape[2]), lambda step: (layer, step // every, 0)))
        shapes.append(((rows, m.shape[2]), jax.ShapeDtypeStruct(m.shape[1:], bf16)))
    return _SideJob(
        steps=steps,
        body=functools.partial(_cast_body, len(arrays), every),
        inputs=tuple(arrays),
        in_specs=tuple(specs),
        out_shapes=tuple(sh for _, sh in shapes),
        out_specs=tuple((blk, lambda step: (step // every, 0)) for blk, _ in shapes),
        scratch=())


def _gmlp_body(step, u_ref, v_ref, z_ref, lnw_ref, lnb_ref, ws_ref, bias_ref, o_ref,
               vn_ref, wm_ref):
    del step
    rows = u_ref.shape[0]

    def norm_part(c):
        rs = slice(c * SG_CHUNK, (c + 1) * SG_CHUNK)
        if c == 0:
            t_idx = lax.broadcasted_iota(jnp.int32, (SG_CHUNK, SG_CHUNK), 0)
            s_idx = lax.broadcasted_iota(jnp.int32, (SG_CHUNK, SG_CHUNK), 1)
            for h in range(SG_HEADS):
                wm_ref[h] = jnp.where(s_idx <= t_idx, ws_ref[h], 0.0).astype(bf16)
        v = v_ref[rs, :].astype(f32)
        mu = jnp.mean(v, axis=-1, keepdims=True)
        vc = v - mu
        var = jnp.mean(vc * vc, axis=-1, keepdims=True)
        vn_ref[rs, :] = (vc * lax.rsqrt(var + EPS) * lnw_ref[...] + lnb_ref[...]).astype(bf16)

    def mix_part(c, h):
        rs = slice(c * SG_CHUNK, (c + 1) * SG_CHUNK)
        cs = slice(h * LANES, (h + 1) * LANES)
        mixed = jnp.dot(wm_ref[h], vn_ref[rs, cs], preferred_element_type=f32)
        mixed = mixed + bias_ref[:, cs]
        o_ref[rs, cs] = (u_ref[rs, cs].astype(f32) * mixed
                         * z_ref[rs, cs].astype(f32)).astype(o_ref.dtype)

    parts = []
    for c in range(rows // SG_CHUNK):
        parts.append(functools.partial(norm_part, c))
        parts.extend(functools.partial(mix_part, c, h) for h in range(SG_HEADS))
    return parts


def _gmlp_job(p, ln_w, ln_b, w_s, b_s, rows=512):
    t = p.shape[0]
    w = SG_WIDTH
    bias = jnp.repeat(b_s.T, LANES, axis=1)
    const2 = lambda step: (0, 0)
    return _SideJob(
        steps=t // rows,
        body=_gmlp_body,
        inputs=(p, p, p, ln_w.reshape(1, w), ln_b.reshape(1, w), w_s, bias),
        in_specs=(((rows, w), lambda step: (step, 0)),
                  ((rows, w), lambda step: (step, 1)),
                  ((rows, w), lambda step: (step, 2)),
                  ((1, w), const2),
                  ((1, w), const2),
                  (w_s.shape, lambda step: (0, 0, 0)),
                  ((SG_CHUNK, w), const2)),
        out_shapes=(jax.ShapeDtypeStruct((t, w), bf16),),
        out_specs=(((rows, w), lambda step: (step, 0)),),
        scratch=(pltpu.VMEM((rows, w), bf16),
                 pltpu.VMEM((SG_HEADS, SG_CHUNK, SG_CHUNK), bf16)))


def _rope(x, cos, sin_up, sin_dn):
    half = ROT_DIM // 2
    return (x * cos + pltpu.roll(x, LANES - half, 1) * sin_up
            + pltpu.roll(x, half, 1) * sin_dn)


def _attn_body(nb, step, sink_ref, q_ref, z_ref, kv_ref, rope_ref, o_ref):
    n = step % nb
    blk = ATT_BLOCK
    group = ATT_HEADS // ATT_KV_HEADS
    pre = {}

    def prepare():
        back = jnp.minimum(n, 1)
        rows_c = pl.ds(pl.multiple_of(step * blk, blk), blk)
        rows_p = pl.ds(pl.multiple_of((step - back) * blk, blk), blk)
        pos_c = pl.ds(pl.multiple_of(n * blk, blk), blk)
        pos_p = pl.ds(pl.multiple_of((n - back) * blk, blk), blk)
        cos_c, up_c, dn_c = rope_ref[0, pos_c, :], rope_ref[1, pos_c, :], rope_ref[2, pos_c, :]
        cos_p, up_p, dn_p = rope_ref[0, pos_p, :], rope_ref[1, pos_p, :], rope_ref[2, pos_p, :]
        k_all = jnp.concatenate([
            _rope(kv_ref[rows_p, :LANES].astype(f32), cos_p, up_p, dn_p),
            _rope(kv_ref[rows_c, :LANES].astype(f32), cos_c, up_c, dn_c)], axis=0)
        v_all = jnp.concatenate([kv_ref[rows_p, LANES:].astype(f32),
                                 kv_ref[rows_c, LANES:].astype(f32)], axis=0)
        lane = lax.broadcasted_iota(jnp.int32, (2 * blk, LANES), 1)
        low = lane < HEAD_DIM
        k_sw = pltpu.roll(k_all, HEAD_DIM, 1)
        v_sw = pltpu.roll(v_all, HEAD_DIM, 1)
        pre["k"] = [[jnp.where(low, k_all, 0.0).astype(bf16),
                     jnp.where(low, 0.0, k_sw).astype(bf16)],
                    [jnp.where(low, k_sw, 0.0).astype(bf16),
                     jnp.where(low, 0.0, k_all).astype(bf16)]]
        pre["v"] = [[jnp.where(low, v_all, 0.0).astype(bf16),
                     jnp.where(low, 0.0, v_sw).astype(bf16)],
                    [jnp.where(low, v_sw, 0.0).astype(bf16),
                     jnp.where(low, 0.0, v_all).astype(bf16)]]
        qi = lax.broadcasted_iota(jnp.int32, (blk, blk), 0)
        kj = lax.broadcasted_iota(jnp.int32, (blk, blk), 1)
        pre["in_cur"] = kj <= qi
        pre["prev_bias"] = jnp.where(pre["in_cur"] | (n > 0), 0.0, NEG_INF)
        scale = HEAD_DIM ** -0.5
        pre["rope_q"] = (cos_c * scale, up_c * scale, dn_c * scale)

    def staged(tile_lo, tile_hi):
        heads = range(2 * tile_lo, 2 * tile_hi)
        tiles = {j: slice(j * LANES, (j + 1) * LANES) for j in range(tile_lo, tile_hi)}
        st = {}

        def scores():
            qs = {j: _rope(q_ref[:, cs].astype(f32), *pre["rope_q"]).astype(bf16)
                  for j, cs in tiles.items()}
            s2 = {hd: lax.dot_general(qs[hd // 2], pre["k"][hd // group][hd % 2],
                                      (((1,), (1,)), ((), ())), preferred_element_type=f32)
                  for hd in heads}
            st["s"] = {hd: jnp.where(pre["in_cur"], s2[hd][:, blk:], s2[hd][:, :blk])
                       + pre["prev_bias"] for hd in heads}

        def row_max():
            st["m"] = {hd: jnp.maximum(jnp.max(st["s"][hd], axis=-1, keepdims=True), sink_ref[hd])
                       for hd in heads}

        def exponent():
            st["p"] = {hd: jnp.exp(st["s"][hd] - st["m"][hd]) for hd in heads}
            st["den"] = {hd: jnp.sum(st["p"][hd], axis=-1, keepdims=True)
                         + jnp.exp(sink_ref[hd] - st["m"][hd]) for hd in heads}

        def values():
            p, in_cur = st["p"], pre["in_cur"]
            p2 = {hd: jnp.concatenate([jnp.where(in_cur, 0.0, p[hd]),
                                       jnp.where(in_cur, p[hd], 0.0)], axis=1).astype(bf16)
                  for hd in heads}
            st["pv"] = {hd: jnp.dot(p2[hd], pre["v"][hd // group][hd % 2],
                                    preferred_element_type=f32) * (1.0 / st["den"][hd])
                        for hd in heads}

        def output():
            pv = st["pv"]
            for j, cs in tiles.items():
                o_ref[:, cs] = ((pv[2 * j] + pv[2 * j + 1])
                                * z_ref[:, cs].astype(f32)).astype(o_ref.dtype)

        return [scores, row_max, exponent, values, output]

    n_tiles = ATT_HEADS // 2
    bounds = [0, 3, 6, n_tiles]
    return [prepare] + [stage for lo, hi in zip(bounds[:-1], bounds[1:])
                        for stage in staged(lo, hi)]


def _rope_tables(seq):
    half = ROT_DIM // 2
    inv_freq = ROPE_THETA ** (-jnp.arange(0, ROT_DIM, 2, dtype=f32) / ROT_DIM)
    ang = jnp.arange(seq, dtype=f32)[:, None] * inv_freq[None, :]
    cos, sin = jnp.cos(ang), jnp.sin(ang)
    ones = jnp.ones((seq, HEAD_DIM - ROT_DIM), f32)
    zeros = jnp.zeros((seq, HEAD_DIM - ROT_DIM), f32)
    zh = jnp.zeros((seq, half), f32)
    cos_t = jnp.concatenate([cos, cos, ones], axis=1)
    up_t = jnp.concatenate([-sin, zh, zeros], axis=1)
    dn_t = jnp.concatenate([zh, sin, zeros], axis=1)
    tab = jnp.stack([cos_t, up_t, dn_t])
    return jnp.concatenate([tab, tab], axis=2)


def _attn_job(p, kv, sinks, rope, bsz, seq):
    nb = seq // ATT_BLOCK
    w = ATT_WIDTH
    return _SideJob(
        steps=bsz * nb,
        body=functools.partial(_attn_body, nb),
        inputs=(sinks, p, p, kv, rope),
        in_specs=(None,
                  ((ATT_BLOCK, w), lambda step: (step, 3)),
                  ((ATT_BLOCK, w), lambda step: (step, 4)),
                  (kv.shape, lambda step: (0, 0)),
                  (rope.shape, lambda step: (0, 0, 0))),
        out_shapes=(jax.ShapeDtypeStruct((bsz * seq, w), bf16),),
        out_specs=(((ATT_BLOCK, w), lambda step: (step, 0)),),
        scratch=())


def _merge_kernel(ya_ref, yb_ref, yc_ref, g_ref, x_ref, wa_ref, wb_ref, wc_ref, wo_ref,
                  nw_ref, *out_refs, last, chunk):
    merged_ref = out_refs[-1]
    ys = (ya_ref, yb_ref, yc_ref)
    ws = (wa_ref, wb_ref, wc_ref)
    for c in range(D_MODEL // chunk):
        cs = slice(c * chunk, (c + 1) * chunk)
        acc = None
        for k in range(N_BRANCH):
            gate = jax.nn.sigmoid(g_ref[:, k * D_MODEL + c * chunk:k * D_MODEL + (c + 1) * chunk]
                                  .astype(f32))
            term = gate * jnp.dot(ys[k][...], ws[k][:, cs], preferred_element_type=f32)
            acc = term if acc is None else acc + term
        merged_ref[:, cs] = acc.astype(bf16)
    x_new = x_ref[...] + jnp.dot(merged_ref[...], wo_ref[...], preferred_element_type=f32)
    ms = jnp.mean(x_new * x_new, axis=-1, keepdims=True)
    normed = x_new * lax.rsqrt(ms + EPS) * nw_ref[...]
    if last:
        out_refs[0][...] = normed
    else:
        out_refs[0][...] = x_new
        out_refs[1][...] = normed.astype(bf16)


def _merge_out(ya_tm, yb, yc, gates, x, wa, wb, wc, wo, next_norm_w, bsz, seq, last, tm=256):
    t = bsz * seq
    nt = seq // tm
    d = D_MODEL
    row = lambda b, n: (b * nt + n, 0)
    const = lambda b, n: (0, 0)
    once = pl.Buffered(1)
    wspec = lambda m: pl.BlockSpec(m.shape, const, pipeline_mode=once)
    if last:
        out_shape = [jax.ShapeDtypeStruct((t, d), f32)]
        out_specs = [pl.BlockSpec((tm, d), row)]
    else:
        out_shape = [jax.ShapeDtypeStruct((t, d), f32), jax.ShapeDtypeStruct((t, d), bf16)]
        out_specs = [pl.BlockSpec((tm, d), row), pl.BlockSpec((tm, d), row)]
    return pl.pallas_call(
        functools.partial(_merge_kernel, last=last, chunk=512),
        grid=(bsz, nt),
        in_specs=[pl.BlockSpec((tm, SSM_WIDTH), lambda b, n: (n, b)),
                  pl.BlockSpec((tm, SG_WIDTH), row),
                  pl.BlockSpec((tm, ATT_WIDTH), row),
                  pl.BlockSpec((tm, N_BRANCH * d), row),
                  pl.BlockSpec((tm, d), row),
                  wspec(wa), wspec(wb), wspec(wc), wspec(wo),
                  pl.BlockSpec((1, d), const)],
        out_specs=out_specs,
        out_shape=out_shape,
        scratch_shapes=[pltpu.VMEM((tm, d), bf16)],
        compiler_params=_cparams("parallel", "parallel"),
        name="merge_out",
    )(ya_tm, yb, yc, gates, x, wa, wb, wc, wo, next_norm_w.reshape(1, d))


def kernel(x, norm_w, w_in, ssm_a_re, ssm_a_im, ssm_log_dt, ssm_b_re, ssm_b_im, ssm_c_re, ssm_c_im, ssm_d, ssm_glu_w, ssm_glu_b, sg_ln_w, sg_ln_b, sg_w, sg_b, attn_sinks, w_branch_a, w_branch_b, w_branch_c, w_out, final_norm_w):
    bsz, seq, d = x.shape
    assert 2 * bsz == SUBLANES, "the S5 kernel packs two timesteps of all batches per sublane tile"
    depth = norm_w.shape[0]
    t = bsz * seq
    xf = x.reshape(t, d)
    rope = _rope_tables(seq)

    o_ub = 2 * SSM_WIDTH
    o_q = o_ub + 3 * SG_WIDTH
    o_k = o_q + ATT_WIDTH
    o_zc = o_k + 2 * KV_WIDTH
    o_g = o_zc + ATT_WIDTH

    tn = 1024
    gate_tn = 768
    s5_cols = [0, SSM_WIDTH]
    bc_cols = [o_ub, o_ub + tn, o_ub + 2 * tn, o_q, o_zc]
    gate_cols = [o_g + c * gate_tn for c in range(N_BRANCH * d // gate_tn)]

    b4s, lam2s, c2s = jax.vmap(_s5_params)(ssm_a_re, ssm_a_im, ssm_log_dt, ssm_b_re,
                                           ssm_b_im, ssm_c_re, ssm_c_im)
    layer_weights = (ssm_glu_w, w_branch_a, w_branch_b, w_branch_c, w_out)

    h = _rmsnorm(xf, norm_w[0], bf16)
    out = None
    for l in range(depth):
        p_bc, gw, wa, wb, wc, wo = _project(
            h, w_in, l, bc_cols, tn,
            side=_cast_job(layer_weights, l, steps=len(bc_cols) * (t // 1024), every=len(bc_cols)),
            activations=["gelu", "gelu", "silu", None, "silu"], name="proj_bc")
        kv = _project(h, w_in, l, [o_k], 2 * KV_WIDTH, name="proj_kv")
        uz, yb = _project(h, w_in, l, s5_cols, tn, time_major_batches=bsz,
                          side=_gmlp_job(p_bc, sg_ln_w[l], sg_ln_b[l], sg_w[l], sg_b[l]),
                          activations=[None, "silu"], name="proj_s5_gmlp")
        gates, yc = _project(h, w_in, l, gate_cols, gate_tn,
                             side=_attn_job(p_bc, kv, attn_sinks[l], rope, bsz, seq),
                             name="proj_gates_attn")
        ya = _s5_mixer(uz, b4s[l], lam2s[l], c2s[l], ssm_d[l], gw, ssm_glu_b[l], bsz)

        last = l == depth - 1
        next_w = final_norm_w if last else norm_w[l + 1]
        res = _merge_out(ya, yb, yc, gates, xf, wa, wb, wc, wo, next_w, bsz, seq, last)
        if last:
            out = res[0]
        else:
            xf, h = res
    return out.reshape(bsz, seq, d)
```

```python
import functools
import math
from typing import Callable, NamedTuple

import jax
import jax.numpy as jnp
from jax import lax
from jax.experimental import pallas as pl
from jax.experimental.pallas import tpu as pltpu

f32 = jnp.float32
bf16 = jnp.bfloat16

D_MODEL = 2048
EPS = 1e-6
NEG_INF = -1e30

SSM_WIDTH = D_MODEL // 2
SSM_GROUP = 16
SSM_GROUPS = SSM_WIDTH // SSM_GROUP
SSM_STATE = 64

SG_WIDTH = D_MODEL // 2
SG_HEADS = 8
SG_CHUNK = 128

HEAD_DIM = 64
ATT_HEADS = D_MODEL // 128
ATT_KV_HEADS = ATT_HEADS // 8
ATT_WIDTH = ATT_HEADS * HEAD_DIM
KV_WIDTH = ATT_KV_HEADS * HEAD_DIM
ATT_BLOCK = 128
ROT_DIM = HEAD_DIM // 4
ROPE_THETA = 500000.0
N_BRANCH = 3

LANES = 128
SUBLANES = 8
MXU_DIM = 256
VMEM_LIMIT_BYTES = 48 * 1024 * 1024

S5_GB = LANES // SSM_GROUP
S5_NGB = SSM_GROUPS // S5_GB
S5_HALF = S5_GB * SSM_STATE
S5_SW = 2 * S5_HALF


def _cparams(*sem):
    return pltpu.CompilerParams(dimension_semantics=sem, vmem_limit_bytes=VMEM_LIMIT_BYTES)


def _silu(x):
    return x * jax.nn.sigmoid(x)


def _rmsnorm_kernel(x_ref, w_ref, o_ref):
    x = x_ref[...]
    ms = jnp.mean(x * x, axis=-1, keepdims=True)
    o_ref[...] = (x * lax.rsqrt(ms + EPS) * w_ref[...]).astype(o_ref.dtype)


def _rmsnorm(x, w, out_dtype, tm=512):
    t, d = x.shape
    return pl.pallas_call(
        _rmsnorm_kernel,
        grid=(t // tm,),
        in_specs=[pl.BlockSpec((tm, d), lambda i: (i, 0)),
                  pl.BlockSpec((1, d), lambda i: (0, 0))],
        out_specs=pl.BlockSpec((tm, d), lambda i: (i, 0)),
        out_shape=jax.ShapeDtypeStruct((t, d), out_dtype),
        compiler_params=_cparams("parallel"),
        name="rmsnorm",
    )(x, w.reshape(1, d))


class _SideJob(NamedTuple):
    steps: int
    body: Callable
    inputs: tuple
    in_specs: tuple
    out_shapes: tuple
    out_specs: tuple
    scratch: tuple


_ACTIVATIONS = {None: lambda r: r, "gelu": jax.nn.gelu, "silu": _silu}


def _proj_kernel(offs_ref, h_ref, w_ref, *refs, side_body, n_side_in, n_side_out, activations):
    del offs_ref
    side_in = refs[:n_side_in]
    o_ref = refs[n_side_in]
    side_out = refs[n_side_in + 1:n_side_in + 1 + n_side_out]
    wbf_ref = refs[n_side_in + 1 + n_side_out]
    side_scratch = refs[n_side_in + 2 + n_side_out:]

    @pl.when(pl.program_id(1) == 0)
    def _():
        wbf_ref[...] = w_ref[...].astype(bf16)

    def run(name):
        act = _ACTIVATIONS[name]
        tm, tn = o_ref.shape
        if side_body is None:
            o_ref[...] = act(jnp.dot(h_ref[...], wbf_ref[...],
                                     preferred_element_type=f32)).astype(o_ref.dtype)
            return
        step = pl.program_id(0) * pl.num_programs(1) + pl.program_id(1)
        parts = side_body(step, *side_in, *side_out, *side_scratch)
        pieces = [(rs, cs) for cs in range(0, tn, MXU_DIM) for rs in range(0, tm, MXU_DIM)]
        n_parts, n_gaps = len(parts), len(pieces) - 1
        done = 0
        for k, (rs, cs) in enumerate(pieces):
            o_ref[rs:rs + MXU_DIM, cs:cs + MXU_DIM] = act(jnp.dot(
                h_ref[rs:rs + MXU_DIM, :], wbf_ref[:, cs:cs + MXU_DIM],
                preferred_element_type=f32)).astype(o_ref.dtype)
            upto = min(n_parts, -(-(k + 1) * n_parts // n_gaps))
            for part in parts[done:upto]:
                part()
            done = upto

    blocks_of = {}
    for jb, name in enumerate(activations):
        blocks_of.setdefault(name, []).append(jb)
    if len(blocks_of) == 1:
        run(activations[0])
    else:
        j = pl.program_id(0)
        for name, blocks in blocks_of.items():
            pl.when(functools.reduce(jnp.logical_or, [j == jb for jb in blocks]))(
                functools.partial(run, name))


def _project(h, w_all, layer, col_offsets, tn, tm=1024, time_major_batches=None, side=None,
             activations=None, name="proj"):
    t, k = h.shape
    nj = len(col_offsets)
    ni = t // tm
    assert all(off % LANES == 0 for off in col_offsets)
    assert side is None or side.steps == nj * ni
    offs = jnp.asarray([off // LANES for off in col_offsets], jnp.int32)
    if time_major_batches is None:
        out_shape = (t, nj * tn)
        out_map = lambda j, i, offs_ref: (i, j)
    else:
        seq = t // time_major_batches
        nt = seq // tm
        out_shape = (seq, nj * time_major_batches * tn)
        out_map = lambda j, i, offs_ref: (i % nt, j * time_major_batches + i // nt)

    def step_spec(spec):
        if spec is None:
            return pl.BlockSpec(memory_space=pltpu.SMEM)
        shape, index = spec
        return pl.BlockSpec(shape, lambda j, i, offs_ref: index(j * ni + i))

    side_in_specs = [] if side is None else [step_spec(sp) for sp in side.in_specs]
    side_out_specs = [] if side is None else [step_spec(sp) for sp in side.out_specs]
    side_inputs = () if side is None else side.inputs
    side_out_shapes = [] if side is None else list(side.out_shapes)
    side_scratch = [] if side is None else list(side.scratch)
    grid_spec = pltpu.PrefetchScalarGridSpec(
        num_scalar_prefetch=1,
        grid=(nj, ni),
        in_specs=[pl.BlockSpec((tm, k), lambda j, i, offs_ref: (i, 0)),
                  pl.BlockSpec((pl.Squeezed(), pl.Element(k), pl.Element(tn)),
                               lambda j, i, offs_ref: (layer, 0, offs_ref[j] * LANES))]
                 + side_in_specs,
        out_specs=[pl.BlockSpec((tm, tn), out_map)] + side_out_specs,
        scratch_shapes=[pltpu.VMEM((k, tn), bf16)] + side_scratch)
    res = pl.pallas_call(
        functools.partial(_proj_kernel, side_body=None if side is None else side.body,
                          n_side_in=len(side_in_specs), n_side_out=len(side_out_specs),
                          activations=tuple(activations or [None] * nj)),
        grid_spec=grid_spec,
        out_shape=[jax.ShapeDtypeStruct(out_shape, bf16)] + side_out_shapes,
        compiler_params=_cparams("arbitrary", "arbitrary"),
        name=name,
    )(offs, h, w_all, *side_inputs)
    return res[0] if side is None else res


def _tile_lanes(x, width):
    while width < LANES:
        x = x + pltpu.roll(x, width, 1)
        width *= 2
    return x


def _s5_kernel(u_ref, z_ref, b4_ref, lam2_ref, c2_ref, d_ref, gw_ref, gb_ref, o_ref,
               w2_ref, cm_ref, ubuf, bu2_ref, st2_ref, y_ref, carry_ref):
    i = pl.program_id(0)
    steps = u_ref.shape[0]
    bsz = SUBLANES // 2
    rows = steps * bsz
    w = SSM_WIDTH

    @pl.when(i == 0)
    def _():
        carry_ref[...] = jnp.zeros_like(carry_ref)
        ubuf[:, 0:SUBLANES, :] = jnp.zeros((S5_NGB, SUBLANES, LANES), f32)
        in_diag = (lax.broadcasted_iota(jnp.int32, (LANES, S5_HALF), 0) // SSM_GROUP
                   == lax.broadcasted_iota(jnp.int32, (LANES, S5_HALF), 1) // SSM_STATE)
        out_diag = (lax.broadcasted_iota(jnp.int32, (S5_HALF, LANES), 0) // SSM_STATE
                    == lax.broadcasted_iota(jnp.int32, (S5_HALF, LANES), 1) // SSM_GROUP)
        for g in range(S5_NGB):
            for k in range(4):
                x = _tile_lanes(b4_ref[k, g], SSM_STATE)
                x = jnp.concatenate([x] * (S5_HALF // LANES), axis=1)
                w2_ref[g, (k // 2) * LANES:(k // 2 + 1) * LANES,
                       (k % 2) * S5_HALF:(k % 2 + 1) * S5_HALF] = (
                           jnp.where(in_diag, x, 0.0).astype(bf16))
            for k in range(2):
                x = _tile_lanes(c2_ref[k, g], SSM_GROUP)
                cm_ref[g, k * S5_HALF:(k + 1) * S5_HALF, :] = (
                    jnp.where(out_diag, x, 0.0).astype(bf16))

    @pl.when(i > 0)
    def _():
        ubuf[:, 0:SUBLANES, :] = ubuf[:, rows:rows + SUBLANES, :]

    for g in range(S5_NGB):
        cs = slice(g * LANES, (g + 1) * LANES)
        for b in range(bsz):
            ubuf[g, pl.ds(SUBLANES + b, steps, stride=bsz), :] = (
                u_ref[:, b * w + g * LANES:b * w + (g + 1) * LANES].astype(f32))
        lhs = jnp.concatenate(
            [ubuf[g, SUBLANES:rows + SUBLANES, :].astype(bf16),
             ubuf[g, SUBLANES - bsz:rows + SUBLANES - bsz, :].astype(bf16)], axis=1)
        bu_ref = bu2_ref.at[g % 2]
        st_ref = st2_ref.at[g % 2]
        bu_ref[...] = jnp.dot(lhs, w2_ref[g], preferred_element_type=f32)
        l_re = lam2_ref[g, :, :S5_HALF]
        l_im = lam2_ref[g, :, S5_HALF:]

        def body(r, s, l_re=l_re, l_im=l_im):
            off = pl.multiple_of(r * 2 * SUBLANES, 2 * SUBLANES)
            b = bu_ref[pl.ds(off, 2 * SUBLANES), :]
            new = []
            for half in range(2):
                s_re, s_im = s
                bh = b[half * SUBLANES:(half + 1) * SUBLANES]
                s = (l_re * s_re - l_im * s_im + bh[:, :S5_HALF],
                     l_re * s_im + l_im * s_re + bh[:, S5_HALF:])
                new.append(s)
            st_ref[pl.ds(off, 2 * SUBLANES), :S5_HALF] = jnp.concatenate(
                [new[0][0], new[1][0]], axis=0).astype(bf16)
            st_ref[pl.ds(off, 2 * SUBLANES), S5_HALF:] = jnp.concatenate(
                [new[0][1], new[1][1]], axis=0).astype(bf16)
            return s

        s0 = (carry_ref[g, :, :S5_HALF], carry_ref[g, :, S5_HALF:])
        s_re, s_im = lax.fori_loop(0, rows // (2 * SUBLANES), body, s0, unroll=True)
        carry_ref[g, :, :S5_HALF] = s_re
        carry_ref[g, :, S5_HALF:] = s_im
        y_g = jnp.dot(st_ref[...], cm_ref[g], preferred_element_type=f32)
        y_ref[g] = y_g + d_ref[:, cs] * ubuf[g, SUBLANES:rows + SUBLANES, :]

    y = jax.nn.gelu(jnp.concatenate([y_ref[g] for g in range(S5_NGB)], axis=1))
    gate = jnp.dot(y.astype(bf16), gw_ref[...], preferred_element_type=f32) + gb_ref[...]
    y = y * jax.nn.sigmoid(gate)
    for g in range(S5_NGB):
        y_ref[g] = y[:, g * LANES:(g + 1) * LANES]
    for b in range(bsz):
        for g in range(S5_NGB):
            cs = slice(b * w + g * LANES, b * w + (g + 1) * LANES)
            o_ref[:, cs] = (y_ref[g, pl.ds(b, steps, stride=bsz), :]
                            * z_ref[:, cs].astype(f32)).astype(o_ref.dtype)


def _s5_params(a_re, a_im, log_dt, b_re, b_im, c_re, c_im):
    dt = jnp.exp(log_dt)[:, None]
    mag = jnp.exp(a_re * dt)
    lb_re = mag * jnp.cos(a_im * dt)
    lb_im = mag * jnp.sin(a_im * dt)
    den = a_re * a_re + a_im * a_im
    k_re = ((lb_re - 1.0) * a_re + lb_im * a_im) / den
    k_im = (lb_im * a_re - (lb_re - 1.0) * a_im) / den
    bb_re = k_re[..., None] * b_re - k_im[..., None] * b_im
    bb_im = k_re[..., None] * b_im + k_im[..., None] * b_re
    t_re = lb_re[..., None] * bb_re - lb_im[..., None] * bb_im
    t_im = lb_re[..., None] * bb_im + lb_im[..., None] * bb_re

    def pad_lanes(m):
        return jnp.pad(m, ((0, 0), (0, 0), (0, LANES - m.shape[-1])))

    b4 = jnp.stack([pad_lanes(m.transpose(0, 2, 1).reshape(S5_NGB, LANES, SSM_STATE))
                    for m in (bb_re, bb_im, t_re, t_im)])
    c2 = jnp.stack([pad_lanes(m.transpose(0, 2, 1).reshape(S5_NGB, S5_HALF, SSM_GROUP))
                    for m in (c_re, -c_im)])
    l2_re = (lb_re * lb_re - lb_im * lb_im).reshape(S5_NGB, 1, S5_HALF)
    l2_im = (2.0 * lb_re * lb_im).reshape(S5_NGB, 1, S5_HALF)
    lam2 = jnp.broadcast_to(jnp.concatenate([l2_re, l2_im], axis=2),
                            (S5_NGB, SUBLANES, S5_SW))
    return b4, lam2, c2


def _s5_mixer(uz, b4, lam2, c2, d, glu_w, glu_b, bsz, steps=128):
    seq = uz.shape[0]
    w = SSM_WIDTH
    bw = bsz * w
    rows = steps * bsz
    const4 = lambda i: (0, 0, 0, 0)
    const3 = lambda i: (0, 0, 0)
    const2 = lambda i: (0, 0)
    return pl.pallas_call(
        _s5_kernel,
        grid=(seq // steps,),
        in_specs=[pl.BlockSpec((steps, bw), lambda i: (i, 0)),
                  pl.BlockSpec((steps, bw), lambda i: (i, 1)),
                  pl.BlockSpec(b4.shape, const4),
                  pl.BlockSpec(lam2.shape, const3),
                  pl.BlockSpec(c2.shape, const4),
                  pl.BlockSpec((1, w), const2),
                  pl.BlockSpec((w, w), const2),
                  pl.BlockSpec((1, w), const2)],
        out_specs=pl.BlockSpec((steps, bw), lambda i: (i, 0)),
        out_shape=jax.ShapeDtypeStruct((seq, bw), bf16),
        scratch_shapes=[pltpu.VMEM((S5_NGB, 2 * LANES, S5_SW), bf16),
                        pltpu.VMEM((S5_NGB, S5_SW, LANES), bf16),
                        pltpu.VMEM((S5_NGB, rows + SUBLANES, LANES), f32),
                        pltpu.VMEM((2, rows, S5_SW), f32),
                        pltpu.VMEM((2, rows, S5_SW), bf16),
                        pltpu.VMEM((S5_NGB, rows, LANES), f32),
                        pltpu.VMEM((S5_NGB, SUBLANES, S5_SW), f32)],
        compiler_params=_cparams("arbitrary"),
        name="s5_mixer",
    )(uz, uz, b4, lam2, c2, d.reshape(1, w), glu_w, glu_b.reshape(1, w))


def _cast_body(n_arrays, step, *refs):
    del step

    def cast():
        for src, dst in zip(refs[:n_arrays], refs[n_arrays:]):
            dst[...] = src[...].astype(dst.dtype)

    return [cast]


def _cast_job(arrays, layer, steps, n_blocks=32):
    block = lambda step: (jnp.minimum(step, n_blocks - 1), 0)
    specs, shapes = [], []
    for m in arrays:
        rows = m.shape[1] // n_blocks
        assert rows * n_blocks == m.shape[1] and rows % (2 * SUBLANES) == 0
        specs.append(((pl.Squeezed(), rows, m.shape[2]),
                      lambda step: (layer, jnp.minimum(step, n_blocks - 1), 0)))
        shapes.append(((rows, m.shape[2]), jax.ShapeDtypeStruct(m.shape[1:], bf16)))
    return _SideJob(
        steps=steps,
        body=functools.partial(_cast_body, len(arrays)),
        inputs=tuple(arrays),
        in_specs=tuple(specs),
        out_shapes=tuple(sh for _, sh in shapes),
        out_specs=tuple((blk, block) for blk, _ in shapes),
        scratch=())


def _gmlp_body(step, u_ref, v_ref, z_ref, lnw_ref, lnb_ref, ws_ref, bias_ref, o_ref,
               vn_ref, wm_ref):
    del step
    rows = u_ref.shape[0]

    def norm_part(c):
        rs = slice(c * SG_CHUNK, (c + 1) * SG_CHUNK)
        if c == 0:
            t_idx = lax.broadcasted_iota(jnp.int32, (SG_CHUNK, SG_CHUNK), 0)
            s_idx = lax.broadcasted_iota(jnp.int32, (SG_CHUNK, SG_CHUNK), 1)
            for h in range(SG_HEADS):
                wm_ref[h] = jnp.where(s_idx <= t_idx, ws_ref[h], 0.0).astype(bf16)
        v = v_ref[rs, :].astype(f32)
        mu = jnp.mean(v, axis=-1, keepdims=True)
        vc = v - mu
        var = jnp.mean(vc * vc, axis=-1, keepdims=True)
        vn_ref[rs, :] = (vc * lax.rsqrt(var + EPS) * lnw_ref[...] + lnb_ref[...]).astype(bf16)

    def mix_part(c, h):
        rs = slice(c * SG_CHUNK, (c + 1) * SG_CHUNK)
        cs = slice(h * LANES, (h + 1) * LANES)
        mixed = jnp.dot(wm_ref[h], vn_ref[rs, cs], preferred_element_type=f32)
        mixed = mixed + bias_ref[:, cs]
        o_ref[rs, cs] = (u_ref[rs, cs].astype(f32) * mixed
                         * z_ref[rs, cs].astype(f32)).astype(o_ref.dtype)

    parts = []
    for c in range(rows // SG_CHUNK):
        parts.append(functools.partial(norm_part, c))
        parts.extend(functools.partial(mix_part, c, h) for h in range(SG_HEADS))
    return parts


def _gmlp_job(p, ln_w, ln_b, w_s, b_s, rows=512):
    t = p.shape[0]
    w = SG_WIDTH
    bias = jnp.repeat(b_s.T, LANES, axis=1)
    const2 = lambda step: (0, 0)
    return _SideJob(
        steps=t // rows,
        body=_gmlp_body,
        inputs=(p, p, p, ln_w.reshape(1, w), ln_b.reshape(1, w), w_s, bias),
        in_specs=(((rows, w), lambda step: (step, 0)),
                  ((rows, w), lambda step: (step, 1)),
                  ((rows, w), lambda step: (step, 2)),
                  ((1, w), const2),
                  ((1, w), const2),
                  (w_s.shape, lambda step: (0, 0, 0)),
                  ((SG_CHUNK, w), const2)),
        out_shapes=(jax.ShapeDtypeStruct((t, w), bf16),),
        out_specs=(((rows, w), lambda step: (step, 0)),),
        scratch=(pltpu.VMEM((rows, w), bf16),
                 pltpu.VMEM((SG_HEADS, SG_CHUNK, SG_CHUNK), bf16)))


def _rope(x, cos, sin_up, sin_dn):
    half = ROT_DIM // 2
    return (x * cos + pltpu.roll(x, LANES - half, 1) * sin_up
            + pltpu.roll(x, half, 1) * sin_dn)


def _attn_body(nb, step, sink_ref, q_ref, z_ref, kvp_ref, kvc_ref, rp_ref, rc_ref, o_ref):
    n = step % nb
    blk = ATT_BLOCK
    group = ATT_HEADS // ATT_KV_HEADS
    pre = {}

    def prepare():
        cos_c, up_c, dn_c = rc_ref[0], rc_ref[1], rc_ref[2]
        cos_p, up_p, dn_p = rp_ref[0], rp_ref[1], rp_ref[2]
        k_all = jnp.concatenate([
            _rope(kvp_ref[:, :LANES].astype(f32), cos_p, up_p, dn_p),
            _rope(kvc_ref[:, :LANES].astype(f32), cos_c, up_c, dn_c)], axis=0)
        v_all = jnp.concatenate([kvp_ref[:, LANES:].astype(f32),
                                 kvc_ref[:, LANES:].astype(f32)], axis=0)
        lane = lax.broadcasted_iota(jnp.int32, (2 * blk, LANES), 1)
        low = lane < HEAD_DIM
        k_sw = pltpu.roll(k_all, HEAD_DIM, 1)
        v_sw = pltpu.roll(v_all, HEAD_DIM, 1)
        pre["k"] = [[jnp.where(low, k_all, 0.0).astype(bf16),
                     jnp.where(low, 0.0, k_sw).astype(bf16)],
                    [jnp.where(low, k_sw, 0.0).astype(bf16),
                     jnp.where(low, 0.0, k_all).astype(bf16)]]
        pre["v"] = [[jnp.where(low, v_all, 0.0).astype(bf16),
                     jnp.where(low, 0.0, v_sw).astype(bf16)],
                    [jnp.where(low, v_sw, 0.0).astype(bf16),
                     jnp.where(low, 0.0, v_all).astype(bf16)]]
        qi = lax.broadcasted_iota(jnp.int32, (blk, blk), 0)
        kj = lax.broadcasted_iota(jnp.int32, (blk, blk), 1)
        pre["in_cur"] = kj <= qi
        pre["prev_bias"] = jnp.where(pre["in_cur"] | (n > 0), 0.0, NEG_INF)
        scale = HEAD_DIM ** -0.5
        pre["rope_q"] = (cos_c * scale, up_c * scale, dn_c * scale)

    def staged(tile_lo, tile_hi):
        heads = range(2 * tile_lo, 2 * tile_hi)
        tiles = {j: slice(j * LANES, (j + 1) * LANES) for j in range(tile_lo, tile_hi)}
        st = {}

        def scores():
            qs = {j: _rope(q_ref[:, cs].astype(f32), *pre["rope_q"]).astype(bf16)
                  for j, cs in tiles.items()}
            s2 = {hd: lax.dot_general(qs[hd // 2], pre["k"][hd // group][hd % 2],
                                      (((1,), (1,)), ((), ())), preferred_element_type=f32)
                  for hd in heads}
            st["s"] = {hd: jnp.where(pre["in_cur"], s2[hd][:, blk:], s2[hd][:, :blk])
                       + pre["prev_bias"] for hd in heads}

        def row_max():
            st["m"] = {hd: jnp.maximum(jnp.max(st["s"][hd], axis=-1, keepdims=True), sink_ref[hd])
                       for hd in heads}

        def exponent():
            st["p"] = {hd: jnp.exp(st["s"][hd] - st["m"][hd]) for hd in heads}
            st["den"] = {hd: jnp.sum(st["p"][hd], axis=-1, keepdims=True)
                         + jnp.exp(sink_ref[hd] - st["m"][hd]) for hd in heads}

        def values():
            p, in_cur = st["p"], pre["in_cur"]
            p2 = {hd: jnp.concatenate([jnp.where(in_cur, 0.0, p[hd]),
                                       jnp.where(in_cur, p[hd], 0.0)], axis=1).astype(bf16)
                  for hd in heads}
            st["pv"] = {hd: jnp.dot(p2[hd], pre["v"][hd // group][hd % 2],
                                    preferred_element_type=f32) * (1.0 / st["den"][hd])
                        for hd in heads}

        def output():
            pv = st["pv"]
            for j, cs in tiles.items():
                o_ref[:, cs] = ((pv[2 * j] + pv[2 * j + 1])
                                * z_ref[:, cs].astype(f32)).astype(o_ref.dtype)

        return [scores, row_max, exponent, values, output]

    n_tiles = ATT_HEADS // 2
    bounds = [0, 3, 6, n_tiles]
    return [prepare] + [stage for lo, hi in zip(bounds[:-1], bounds[1:])
                        for stage in staged(lo, hi)]


def _rope_tables(seq):
    half = ROT_DIM // 2
    inv_freq = ROPE_THETA ** (-jnp.arange(0, ROT_DIM, 2, dtype=f32) / ROT_DIM)
    ang = jnp.arange(seq, dtype=f32)[:, None] * inv_freq[None, :]
    cos, sin = jnp.cos(ang), jnp.sin(ang)
    ones = jnp.ones((seq, HEAD_DIM - ROT_DIM), f32)
    zeros = jnp.zeros((seq, HEAD_DIM - ROT_DIM), f32)
    zh = jnp.zeros((seq, half), f32)
    cos_t = jnp.concatenate([cos, cos, ones], axis=1)
    up_t = jnp.concatenate([-sin, zh, zeros], axis=1)
    dn_t = jnp.concatenate([zh, sin, zeros], axis=1)
    tab = jnp.stack([cos_t, up_t, dn_t])
    return jnp.concatenate([tab, tab], axis=2)


def _attn_job(p, kv, sinks, rope, bsz, seq):
    nb = seq // ATT_BLOCK
    w = ATT_WIDTH
    prev = lambda step: step - jnp.minimum(step % nb, 1)
    return _SideJob(
        steps=bsz * nb,
        body=functools.partial(_attn_body, nb),
        inputs=(sinks, p, p, kv, kv, rope, rope),
        in_specs=(None,
                  ((ATT_BLOCK, w), lambda step: (step, 3)),
                  ((ATT_BLOCK, w), lambda step: (step, 4)),
                  ((ATT_BLOCK, 2 * KV_WIDTH), lambda step: (prev(step), 0)),
                  ((ATT_BLOCK, 2 * KV_WIDTH), lambda step: (step, 0)),
                  ((3, ATT_BLOCK, LANES), lambda step: (0, prev(step) % nb, 0)),
                  ((3, ATT_BLOCK, LANES), lambda step: (0, step % nb, 0))),
        out_shapes=(jax.ShapeDtypeStruct((bsz * seq, w), bf16),),
        out_specs=(((ATT_BLOCK, w), lambda step: (step, 0)),),
        scratch=())


def _merge_kernel(ya_ref, yb_ref, yc_ref, g_ref, x_ref, wa_ref, wb_ref, wc_ref, wo_ref,
                  nw_ref, *out_refs, last, chunk):
    merged_ref = out_refs[-1]
    ys = (ya_ref, yb_ref, yc_ref)
    ws = (wa_ref, wb_ref, wc_ref)
    for c in range(D_MODEL // chunk):
        cs = slice(c * chunk, (c + 1) * chunk)
        acc = None
        for k in range(N_BRANCH):
            gate = jax.nn.sigmoid(g_ref[:, k * D_MODEL + c * chunk:k * D_MODEL + (c + 1) * chunk]
                                  .astype(f32))
            term = gate * jnp.dot(ys[k][...], ws[k][:, cs], preferred_element_type=f32)
            acc = term if acc is None else acc + term
        merged_ref[:, cs] = acc.astype(bf16)
    x_new = x_ref[...] + jnp.dot(merged_ref[...], wo_ref[...], preferred_element_type=f32)
    ms = jnp.mean(x_new * x_new, axis=-1, keepdims=True)
    normed = x_new * lax.rsqrt(ms + EPS) * nw_ref[...]
    if last:
        out_refs[0][...] = normed
    else:
        out_refs[0][...] = x_new
        out_refs[1][...] = normed.astype(bf16)


def _merge_out(ya_tm, yb, yc, gates, x, wa, wb, wc, wo, next_norm_w, bsz, seq, last, tm=256):
    t = bsz * seq
    nt = seq // tm
    d = D_MODEL
    row = lambda b, n: (b * nt + n, 0)
    const = lambda b, n: (0, 0)
    once = pl.Buffered(1)
    wspec = lambda m: pl.BlockSpec(m.shape, const, pipeline_mode=once)
    if last:
        out_shape = [jax.ShapeDtypeStruct((t, d), f32)]
        out_specs = [pl.BlockSpec((tm, d), row)]
    else:
        out_shape = [jax.ShapeDtypeStruct((t, d), f32), jax.ShapeDtypeStruct((t, d), bf16)]
        out_specs = [pl.BlockSpec((tm, d), row), pl.BlockSpec((tm, d), row)]
    return pl.pallas_call(
        functools.partial(_merge_kernel, last=last, chunk=512),
        grid=(bsz, nt),
        in_specs=[pl.BlockSpec((tm, SSM_WIDTH), lambda b, n: (n, b)),
                  pl.BlockSpec((tm, SG_WIDTH), row),
                  pl.BlockSpec((tm, ATT_WIDTH), row),
                  pl.BlockSpec((tm, N_BRANCH * d), row),
                  pl.BlockSpec((tm, d), row),
                  wspec(wa), wspec(wb), wspec(wc), wspec(wo),
                  pl.BlockSpec((1, d), const)],
        out_specs=out_specs,
        out_shape=out_shape,
        scratch_shapes=[pltpu.VMEM((tm, d), bf16)],
        compiler_params=_cparams("parallel", "parallel"),
        name="merge_out",
    )(ya_tm, yb, yc, gates, x, wa, wb, wc, wo, next_norm_w.reshape(1, d))


def kernel(x, norm_w, w_in, ssm_a_re, ssm_a_im, ssm_log_dt, ssm_b_re, ssm_b_im, ssm_c_re, ssm_c_im, ssm_d, ssm_glu_w, ssm_glu_b, sg_ln_w, sg_ln_b, sg_w, sg_b, attn_sinks, w_branch_a, w_branch_b, w_branch_c, w_out, final_norm_w):
    bsz, seq, d = x.shape
    assert 2 * bsz == SUBLANES, "the S5 kernel packs two timesteps of all batches per sublane tile"
    depth = norm_w.shape[0]
    t = bsz * seq
    xf = x.reshape(t, d)
    rope = _rope_tables(seq)

    o_ub = 2 * SSM_WIDTH
    o_q = o_ub + 3 * SG_WIDTH
    o_k = o_q + ATT_WIDTH
    o_zc = o_k + 2 * KV_WIDTH
    o_g = o_zc + ATT_WIDTH

    tn = 1024
    gate_tn = 768
    s5_cols = [0, SSM_WIDTH]
    bc_cols = [o_ub, o_ub + tn, o_ub + 2 * tn, o_q, o_zc]
    gate_cols = [o_g + c * gate_tn for c in range(N_BRANCH * d // gate_tn)]

    b4s, lam2s, c2s = jax.vmap(_s5_params)(ssm_a_re, ssm_a_im, ssm_log_dt, ssm_b_re,
                                           ssm_b_im, ssm_c_re, ssm_c_im)
    layer_weights = (ssm_glu_w, w_branch_a, w_branch_b, w_branch_c, w_out)

    h = _rmsnorm(xf, norm_w[0], bf16)
    out = None
    for l in range(depth):
        p_bc, gw, wa, wb, wc, wo = _project(
            h, w_in, l, bc_cols, tn,
            side=_cast_job(layer_weights, l, steps=len(bc_cols) * (t // 1024)),
            activations=["gelu", "gelu", "silu", None, "silu"], name="proj_bc")
        kv = _project(h, w_in, l, [o_k], 2 * KV_WIDTH, name="proj_kv")
        uz, yb = _project(h, w_in, l, s5_cols, tn, time_major_batches=bsz,
                          side=_gmlp_job(p_bc, sg_ln_w[l], sg_ln_b[l], sg_w[l], sg_b[l]),
                          activations=[None, "silu"], name="proj_s5_gmlp")
        gates, yc = _project(h, w_in, l, gate_cols, gate_tn,
                             side=_attn_job(p_bc, kv, attn_sinks[l], rope, bsz, seq),
                             name="proj_gates_attn")
        ya = _s5_mixer(uz, b4s[l], lam2s[l], c2s[l], ssm_d[l], gw, ssm_glu_b[l], bsz)

        last = l == depth - 1
        next_w = final_norm_w if last else norm_w[l + 1]
        res = _merge_out(ya, yb, yc, gates, xf, wa, wb, wc, wo, next_w, bsz, seq, last)
        if last:
            out = res[0]
        else:
            xf, h = res
    return out.reshape(bsz, seq, d)
```

```python
import functools
import math
from typing import Callable, NamedTuple

import jax
import jax.numpy as jnp
from jax import lax
from jax.experimental import pallas as pl
from jax.experimental.pallas import tpu as pltpu

f32 = jnp.float32
bf16 = jnp.bfloat16

D_MODEL = 2048
EPS = 1e-6
NEG_INF = -1e30

SSM_WIDTH = D_MODEL // 2
SSM_GROUP = 16
SSM_GROUPS = SSM_WIDTH // SSM_GROUP
SSM_STATE = 64

SG_WIDTH = D_MODEL // 2
SG_HEADS = 8
SG_CHUNK = 128

HEAD_DIM = 64
ATT_HEADS = D_MODEL // 128
ATT_KV_HEADS = ATT_HEADS // 8
ATT_WIDTH = ATT_HEADS * HEAD_DIM
KV_WIDTH = ATT_KV_HEADS * HEAD_DIM
ATT_BLOCK = 128
ROT_DIM = HEAD_DIM // 4
ROPE_THETA = 500000.0
N_BRANCH = 3

LANES = 128
SUBLANES = 8
MXU_DIM = 256
VMEM_LIMIT_BYTES = 48 * 1024 * 1024

S5_GB = LANES // SSM_GROUP
S5_NGB = SSM_GROUPS // S5_GB
S5_HALF = S5_GB * SSM_STATE
S5_SW = 2 * S5_HALF


def _cparams(*sem):
    return pltpu.CompilerParams(dimension_semantics=sem, vmem_limit_bytes=VMEM_LIMIT_BYTES)


def _silu(x):
    return x * jax.nn.sigmoid(x)


def _rmsnorm_kv_kernel(x_ref, w_ref, wkv_ref, h_ref, kv_ref):
    x = x_ref[...]
    ms = jnp.mean(x * x, axis=-1, keepdims=True)
    h = (x * lax.rsqrt(ms + EPS) * w_ref[...]).astype(bf16)
    h_ref[...] = h
    kv_ref[...] = jnp.dot(h, wkv_ref[...], preferred_element_type=f32).astype(bf16)


def _rmsnorm_kv(x, w, wkv, tm=512):
    t, d = x.shape
    n = wkv.shape[1]
    return pl.pallas_call(
        _rmsnorm_kv_kernel,
        grid=(t // tm,),
        in_specs=[pl.BlockSpec((tm, d), lambda i: (i, 0)),
                  pl.BlockSpec((1, d), lambda i: (0, 0)),
                  pl.BlockSpec((d, n), lambda i: (0, 0))],
        out_specs=[pl.BlockSpec((tm, d), lambda i: (i, 0)),
                   pl.BlockSpec((tm, n), lambda i: (i, 0))],
        out_shape=[jax.ShapeDtypeStruct((t, d), bf16), jax.ShapeDtypeStruct((t, n), bf16)],
        compiler_params=_cparams("parallel"),
        name="rmsnorm_kv",
    )(x, w.reshape(1, d), wkv)


class _SideJob(NamedTuple):
    steps: int
    body: Callable
    inputs: tuple
    in_specs: tuple
    out_shapes: tuple
    out_specs: tuple
    scratch: tuple


_ACTIVATIONS = {None: lambda r: r, "gelu": jax.nn.gelu, "silu": _silu}


def _proj_kernel(offs_ref, h_ref, w_ref, *refs, side_body, n_side_in, n_side_out, activations):
    del offs_ref
    side_in = refs[:n_side_in]
    o_ref = refs[n_side_in]
    side_out = refs[n_side_in + 1:n_side_in + 1 + n_side_out]
    wbf_ref = refs[n_side_in + 1 + n_side_out]
    side_scratch = refs[n_side_in + 2 + n_side_out:]

    @pl.when(pl.program_id(1) == 0)
    def _():
        wbf_ref[...] = w_ref[...].astype(bf16)

    def run(name):
        act = _ACTIVATIONS[name]
        tm, tn = o_ref.shape
        if side_body is None:
            o_ref[...] = act(jnp.dot(h_ref[...], wbf_ref[...],
                                     preferred_element_type=f32)).astype(o_ref.dtype)
            return
        step = pl.program_id(0) * pl.num_programs(1) + pl.program_id(1)
        parts = side_body(step, *side_in, *side_out, *side_scratch)
        pieces = [(rs, cs) for cs in range(0, tn, MXU_DIM) for rs in range(0, tm, MXU_DIM)]
        n_parts, n_gaps = len(parts), len(pieces) - 1
        done = 0
        for k, (rs, cs) in enumerate(pieces):
            o_ref[rs:rs + MXU_DIM, cs:cs + MXU_DIM] = act(jnp.dot(
                h_ref[rs:rs + MXU_DIM, :], wbf_ref[:, cs:cs + MXU_DIM],
                preferred_element_type=f32)).astype(o_ref.dtype)
            upto = min(n_parts, -(-(k + 1) * n_parts // n_gaps))
            for part in parts[done:upto]:
                part()
            done = upto

    blocks_of = {}
    for jb, name in enumerate(activations):
        blocks_of.setdefault(name, []).append(jb)
    if len(blocks_of) == 1:
        run(activations[0])
    else:
        j = pl.program_id(0)
        for name, blocks in blocks_of.items():
            pl.when(functools.reduce(jnp.logical_or, [j == jb for jb in blocks]))(
                functools.partial(run, name))


def _project(h, w_all, layer, col_offsets, tn, tm=1024, time_major_batches=None, side=None,
             activations=None, name="proj"):
    t, k = h.shape
    nj = len(col_offsets)
    ni = t // tm
    assert all(off % LANES == 0 for off in col_offsets)
    assert side is None or side.steps == nj * ni
    offs = jnp.asarray([off // LANES for off in col_offsets], jnp.int32)
    if time_major_batches is None:
        out_shape = (t, nj * tn)
        out_map = lambda j, i, offs_ref: (i, j)
    else:
        seq = t // time_major_batches
        nt = seq // tm
        out_shape = (seq, nj * time_major_batches * tn)
        out_map = lambda j, i, offs_ref: (i % nt, j * time_major_batches + i // nt)

    def step_spec(spec):
        if spec is None:
            return pl.BlockSpec(memory_space=pltpu.SMEM)
        shape, index = spec
        return pl.BlockSpec(shape, lambda j, i, offs_ref: index(j * ni + i))

    side_in_specs = [] if side is None else [step_spec(sp) for sp in side.in_specs]
    side_out_specs = [] if side is None else [step_spec(sp) for sp in side.out_specs]
    side_inputs = () if side is None else side.inputs
    side_out_shapes = [] if side is None else list(side.out_shapes)
    side_scratch = [] if side is None else list(side.scratch)
    grid_spec = pltpu.PrefetchScalarGridSpec(
        num_scalar_prefetch=1,
        grid=(nj, ni),
        in_specs=[pl.BlockSpec((tm, k), lambda j, i, offs_ref: (i, 0)),
                  pl.BlockSpec((pl.Squeezed(), pl.Element(k), pl.Element(tn)),
                               lambda j, i, offs_ref: (layer, 0, offs_ref[j] * LANES))]
                 + side_in_specs,
        out_specs=[pl.BlockSpec((tm, tn), out_map)] + side_out_specs,
        scratch_shapes=[pltpu.VMEM((k, tn), bf16)] + side_scratch)
    res = pl.pallas_call(
        functools.partial(_proj_kernel, side_body=None if side is None else side.body,
                          n_side_in=len(side_in_specs), n_side_out=len(side_out_specs),
                          activations=tuple(activations or [None] * nj)),
        grid_spec=grid_spec,
        out_shape=[jax.ShapeDtypeStruct(out_shape, bf16)] + side_out_shapes,
        compiler_params=_cparams("arbitrary", "arbitrary"),
        name=name,
    )(offs, h, w_all, *side_inputs)
    return res[0] if side is None else res


def _tile_lanes(x, width):
    while width < LANES:
        x = x + pltpu.roll(x, width, 1)
        width *= 2
    return x


def _s5_kernel(u_ref, z_ref, b4_ref, lam2_ref, c2_ref, d_ref, gw_ref, gb_ref, o_ref,
               w2_ref, cm_ref, ubuf, bu2_ref, st2_ref, y_ref, carry_ref):
    i = pl.program_id(0)
    steps = u_ref.shape[0]
    bsz = SUBLANES // 2
    rows = steps * bsz
    w = SSM_WIDTH

    @pl.when(i == 0)
    def _():
        carry_ref[...] = jnp.zeros_like(carry_ref)
        ubuf[:, 0:SUBLANES, :] = jnp.zeros((S5_NGB, SUBLANES, LANES), f32)
        in_diag = (lax.broadcasted_iota(jnp.int32, (LANES, S5_HALF), 0) // SSM_GROUP
                   == lax.broadcasted_iota(jnp.int32, (LANES, S5_HALF), 1) // SSM_STATE)
        out_diag = (lax.broadcasted_iota(jnp.int32, (S5_HALF, LANES), 0) // SSM_STATE
                    == lax.broadcasted_iota(jnp.int32, (S5_HALF, LANES), 1) // SSM_GROUP)
        for g in range(S5_NGB):
            for k in range(4):
                x = _tile_lanes(b4_ref[k, g], SSM_STATE)
                x = jnp.concatenate([x] * (S5_HALF // LANES), axis=1)
                w2_ref[g, (k // 2) * LANES:(k // 2 + 1) * LANES,
                       (k % 2) * S5_HALF:(k % 2 + 1) * S5_HALF] = (
                           jnp.where(in_diag, x, 0.0).astype(bf16))
            for k in range(2):
                x = _tile_lanes(c2_ref[k, g], SSM_GROUP)
                cm_ref[g, k * S5_HALF:(k + 1) * S5_HALF, :] = (
                    jnp.where(out_diag, x, 0.0).astype(bf16))

    @pl.when(i > 0)
    def _():
        ubuf[:, 0:SUBLANES, :] = ubuf[:, rows:rows + SUBLANES, :]

    for g in range(S5_NGB):
        cs = slice(g * LANES, (g + 1) * LANES)
        for b in range(bsz):
            ubuf[g, pl.ds(SUBLANES + b, steps, stride=bsz), :] = (
                u_ref[:, b * w + g * LANES:b * w + (g + 1) * LANES].astype(f32))
        lhs = jnp.concatenate(
            [ubuf[g, SUBLANES:rows + SUBLANES, :].astype(bf16),
             ubuf[g, SUBLANES - bsz:rows + SUBLANES - bsz, :].astype(bf16)], axis=1)
        bu_ref = bu2_ref.at[g % 2]
        st_ref = st2_ref.at[g % 2]
        bu_ref[...] = jnp.dot(lhs, w2_ref[g], preferred_element_type=f32)
        l_re = lam2_ref[g, :, :S5_HALF]
        l_im = lam2_ref[g, :, S5_HALF:]

        def body(r, s, l_re=l_re, l_im=l_im):
            s_re, s_im = s
            off = pl.multiple_of(r * SUBLANES, SUBLANES)
            b = bu_ref[pl.ds(off, SUBLANES), :]
            n_re = l_re * s_re - l_im * s_im + b[:, :S5_HALF]
            n_im = l_re * s_im + l_im * s_re + b[:, S5_HALF:]
            st_ref[pl.ds(off, SUBLANES), :S5_HALF] = n_re
            st_ref[pl.ds(off, SUBLANES), S5_HALF:] = n_im
            return n_re, n_im

        s0 = (carry_ref[g, :, :S5_HALF], carry_ref[g, :, S5_HALF:])
        s_re, s_im = lax.fori_loop(0, rows // SUBLANES, body, s0, unroll=True)
        carry_ref[g, :, :S5_HALF] = s_re
        carry_ref[g, :, S5_HALF:] = s_im
        y_g = jnp.dot(st_ref[...].astype(bf16), cm_ref[g], preferred_element_type=f32)
        y_ref[g] = y_g + d_ref[:, cs] * ubuf[g, SUBLANES:rows + SUBLANES, :]

    y = jax.nn.gelu(jnp.concatenate([y_ref[g] for g in range(S5_NGB)], axis=1))
    gate = jnp.dot(y.astype(bf16), gw_ref[...], preferred_element_type=f32) + gb_ref[...]
    y = y * jax.nn.sigmoid(gate)
    for g in range(S5_NGB):
        y_ref[g] = y[:, g * LANES:(g + 1) * LANES]
    for b in range(bsz):
        for g in range(S5_NGB):
            cs = slice(b * w + g * LANES, b * w + (g + 1) * LANES)
            o_ref[:, cs] = (y_ref[g, pl.ds(b, steps, stride=bsz), :]
                            * z_ref[:, cs].astype(f32)).astype(o_ref.dtype)


def _s5_params(a_re, a_im, log_dt, b_re, b_im, c_re, c_im):
    dt = jnp.exp(log_dt)[:, None]
    mag = jnp.exp(a_re * dt)
    lb_re = mag * jnp.cos(a_im * dt)
    lb_im = mag * jnp.sin(a_im * dt)
    den = a_re * a_re + a_im * a_im
    k_re = ((lb_re - 1.0) * a_re + lb_im * a_im) / den
    k_im = (lb_im * a_re - (lb_re - 1.0) * a_im) / den
    bb_re = k_re[..., None] * b_re - k_im[..., None] * b_im
    bb_im = k_re[..., None] * b_im + k_im[..., None] * b_re
    t_re = lb_re[..., None] * bb_re - lb_im[..., None] * bb_im
    t_im = lb_re[..., None] * bb_im + lb_im[..., None] * bb_re

    def pad_lanes(m):
        return jnp.pad(m, ((0, 0), (0, 0), (0, LANES - m.shape[-1])))

    b4 = jnp.stack([pad_lanes(m.transpose(0, 2, 1).reshape(S5_NGB, LANES, SSM_STATE))
                    for m in (bb_re, bb_im, t_re, t_im)])
    c2 = jnp.stack([pad_lanes(m.transpose(0, 2, 1).reshape(S5_NGB, S5_HALF, SSM_GROUP))
                    for m in (c_re, -c_im)])
    l2_re = (lb_re * lb_re - lb_im * lb_im).reshape(S5_NGB, 1, S5_HALF)
    l2_im = (2.0 * lb_re * lb_im).reshape(S5_NGB, 1, S5_HALF)
    lam2 = jnp.broadcast_to(jnp.concatenate([l2_re, l2_im], axis=2),
                            (S5_NGB, SUBLANES, S5_SW))
    return b4, lam2, c2


def _s5_mixer(uz, b4, lam2, c2, d, glu_w, glu_b, bsz, steps=128):
    seq = uz.shape[0]
    w = SSM_WIDTH
    bw = bsz * w
    rows = steps * bsz
    const4 = lambda i: (0, 0, 0, 0)
    const3 = lambda i: (0, 0, 0)
    const2 = lambda i: (0, 0)
    return pl.pallas_call(
        _s5_kernel,
        grid=(seq // steps,),
        in_specs=[pl.BlockSpec((steps, bw), lambda i: (i, 0)),
                  pl.BlockSpec((steps, bw), lambda i: (i, 1)),
                  pl.BlockSpec(b4.shape, const4),
                  pl.BlockSpec(lam2.shape, const3),
                  pl.BlockSpec(c2.shape, const4),
                  pl.BlockSpec((1, w), const2),
                  pl.BlockSpec((w, w), const2),
                  pl.BlockSpec((1, w), const2)],
        out_specs=pl.BlockSpec((steps, bw), lambda i: (i, 0)),
        out_shape=jax.ShapeDtypeStruct((seq, bw), bf16),
        scratch_shapes=[pltpu.VMEM((S5_NGB, 2 * LANES, S5_SW), bf16),
                        pltpu.VMEM((S5_NGB, S5_SW, LANES), bf16),
                        pltpu.VMEM((S5_NGB, rows + SUBLANES, LANES), f32),
                        pltpu.VMEM((2, rows, S5_SW), f32),
                        pltpu.VMEM((2, rows, S5_SW), f32),
                        pltpu.VMEM((S5_NGB, rows, LANES), f32),
                        pltpu.VMEM((S5_NGB, SUBLANES, S5_SW), f32)],
        compiler_params=_cparams("arbitrary"),
        name="s5_mixer",
    )(uz, uz, b4, lam2, c2, d.reshape(1, w), glu_w, glu_b.reshape(1, w))


def _cast_body(n_arrays, step, *refs):
    del step

    def cast():
        for src, dst in zip(refs[:n_arrays], refs[n_arrays:]):
            dst[...] = src[...].astype(dst.dtype)

    return [cast]


def _cast_job(arrays, layer, steps, n_blocks=32):
    block = lambda step: (jnp.minimum(step, n_blocks - 1), 0)
    specs, shapes = [], []
    for m in arrays:
        rows = m.shape[1] // n_blocks
        assert rows * n_blocks == m.shape[1] and rows % (2 * SUBLANES) == 0
        specs.append(((pl.Squeezed(), rows, m.shape[2]),
                      lambda step: (layer, jnp.minimum(step, n_blocks - 1), 0)))
        shapes.append(((rows, m.shape[2]), jax.ShapeDtypeStruct(m.shape[1:], bf16)))
    return _SideJob(
        steps=steps,
        body=functools.partial(_cast_body, len(arrays)),
        inputs=tuple(arrays),
        in_specs=tuple(specs),
        out_shapes=tuple(sh for _, sh in shapes),
        out_specs=tuple((blk, block) for blk, _ in shapes),
        scratch=())


def _gmlp_body(step, u_ref, v_ref, z_ref, lnw_ref, lnb_ref, ws_ref, bias_ref, o_ref,
               vn_ref, wm_ref):
    del step
    rows = u_ref.shape[0]

    def norm_part(c):
        rs = slice(c * SG_CHUNK, (c + 1) * SG_CHUNK)
        if c == 0:
            t_idx = lax.broadcasted_iota(jnp.int32, (SG_CHUNK, SG_CHUNK), 0)
            s_idx = lax.broadcasted_iota(jnp.int32, (SG_CHUNK, SG_CHUNK), 1)
            for h in range(SG_HEADS):
                wm_ref[h] = jnp.where(s_idx <= t_idx, ws_ref[h], 0.0).astype(bf16)
        v = v_ref[rs, :].astype(f32)
        mu = jnp.mean(v, axis=-1, keepdims=True)
        vc = v - mu
        var = jnp.mean(vc * vc, axis=-1, keepdims=True)
        vn_ref[rs, :] = (vc * lax.rsqrt(var + EPS) * lnw_ref[...] + lnb_ref[...]).astype(bf16)

    def mix_part(c, h):
        rs = slice(c * SG_CHUNK, (c + 1) * SG_CHUNK)
        cs = slice(h * LANES, (h + 1) * LANES)
        mixed = jnp.dot(wm_ref[h], vn_ref[rs, cs], preferred_element_type=f32)
        mixed = mixed + bias_ref[:, cs]
        o_ref[rs, cs] = (u_ref[rs, cs].astype(f32) * mixed
                         * z_ref[rs, cs].astype(f32)).astype(o_ref.dtype)

    parts = []
    for c in range(rows // SG_CHUNK):
        parts.append(functools.partial(norm_part, c))
        parts.extend(functools.partial(mix_part, c, h) for h in range(SG_HEADS))
    return parts


def _gmlp_job(p, ln_w, ln_b, w_s, b_s, rows=512):
    t = p.shape[0]
    w = SG_WIDTH
    bias = jnp.repeat(b_s.T, LANES, axis=1)
    const2 = lambda step: (0, 0)
    return _SideJob(
        steps=t // rows,
        body=_gmlp_body,
        inputs=(p, p, p, ln_w.reshape(1, w), ln_b.reshape(1, w), w_s, bias),
        in_specs=(((rows, w), lambda step: (step, 0)),
                  ((rows, w), lambda step: (step, 1)),
                  ((rows, w), lambda step: (step, 2)),
                  ((1, w), const2),
                  ((1, w), const2),
                  (w_s.shape, lambda step: (0, 0, 0)),
                  ((SG_CHUNK, w), const2)),
        out_shapes=(jax.ShapeDtypeStruct((t, w), bf16),),
        out_specs=(((rows, w), lambda step: (step, 0)),),
        scratch=(pltpu.VMEM((rows, w), bf16),
                 pltpu.VMEM((SG_HEADS, SG_CHUNK, SG_CHUNK), bf16)))


def _rope(x, cos, sin_up, sin_dn):
    half = ROT_DIM // 2
    return (x * cos + pltpu.roll(x, LANES - half, 1) * sin_up
            + pltpu.roll(x, half, 1) * sin_dn)


def _attn_body(nb, step, sink_ref, q_ref, z_ref, kvp_ref, kvc_ref, rp_ref, rc_ref, o_ref):
    n = step % nb
    blk = ATT_BLOCK
    group = ATT_HEADS // ATT_KV_HEADS
    pre = {}

    def prepare():
        cos_c, up_c, dn_c = rc_ref[0], rc_ref[1], rc_ref[2]
        cos_p, up_p, dn_p = rp_ref[0], rp_ref[1], rp_ref[2]
        k_all = jnp.concatenate([
            _rope(kvp_ref[:, :LANES].astype(f32), cos_p, up_p, dn_p),
            _rope(kvc_ref[:, :LANES].astype(f32), cos_c, up_c, dn_c)], axis=0)
        v_all = jnp.concatenate([kvp_ref[:, LANES:].astype(f32),
                                 kvc_ref[:, LANES:].astype(f32)], axis=0)
        lane = lax.broadcasted_iota(jnp.int32, (2 * blk, LANES), 1)
        low = lane < HEAD_DIM
        k_sw = pltpu.roll(k_all, HEAD_DIM, 1)
        v_sw = pltpu.roll(v_all, HEAD_DIM, 1)
        pre["k"] = [[jnp.where(low, k_all, 0.0).astype(bf16),
                     jnp.where(low, 0.0, k_sw).astype(bf16)],
                    [jnp.where(low, k_sw, 0.0).astype(bf16),
                     jnp.where(low, 0.0, k_all).astype(bf16)]]
        pre["v"] = [[jnp.where(low, v_all, 0.0).astype(bf16),
                     jnp.where(low, 0.0, v_sw).astype(bf16)],
                    [jnp.where(low, v_sw, 0.0).astype(bf16),
                     jnp.where(low, 0.0, v_all).astype(bf16)]]
        qi = lax.broadcasted_iota(jnp.int32, (blk, blk), 0)
        kj = lax.broadcasted_iota(jnp.int32, (blk, blk), 1)
        pre["in_cur"] = kj <= qi
        pre["prev_bias"] = jnp.where(pre["in_cur"] | (n > 0), 0.0, NEG_INF)
        scale = HEAD_DIM ** -0.5
        pre["rope_q"] = (cos_c * scale, up_c * scale, dn_c * scale)

    def staged(tile_lo, tile_hi):
        heads = range(2 * tile_lo, 2 * tile_hi)
        tiles = {j: slice(j * LANES, (j + 1) * LANES) for j in range(tile_lo, tile_hi)}
        st = {}

        def scores():
            qs = {j: _rope(q_ref[:, cs].astype(f32), *pre["rope_q"]).astype(bf16)
                  for j, cs in tiles.items()}
            s2 = {hd: lax.dot_general(qs[hd // 2], pre["k"][hd // group][hd % 2],
                                      (((1,), (1,)), ((), ())), preferred_element_type=f32)
                  for hd in heads}
            st["s"] = {hd: jnp.where(pre["in_cur"], s2[hd][:, blk:], s2[hd][:, :blk])
                       + pre["prev_bias"] for hd in heads}

        def row_max():
            st["m"] = {hd: jnp.maximum(jnp.max(st["s"][hd], axis=-1, keepdims=True), sink_ref[hd])
                       for hd in heads}

        def exponent():
            st["p"] = {hd: jnp.exp(st["s"][hd] - st["m"][hd]) for hd in heads}
            st["den"] = {hd: jnp.sum(st["p"][hd], axis=-1, keepdims=True)
                         + jnp.exp(sink_ref[hd] - st["m"][hd]) for hd in heads}

        def values():
            p, in_cur = st["p"], pre["in_cur"]
            p2 = {hd: jnp.concatenate([jnp.where(in_cur, 0.0, p[hd]),
                                       jnp.where(in_cur, p[hd], 0.0)], axis=1).astype(bf16)
                  for hd in heads}
            st["pv"] = {hd: jnp.dot(p2[hd], pre["v"][hd // group][hd % 2],
                                    preferred_element_type=f32) * (1.0 / st["den"][hd])
                        for hd in heads}

        def output():
            pv = st["pv"]
            for j, cs in tiles.items():
                o_ref[:, cs] = ((pv[2 * j] + pv[2 * j + 1])
                                * z_ref[:, cs].astype(f32)).astype(o_ref.dtype)

        return [scores, row_max, exponent, values, output]

    n_tiles = ATT_HEADS // 2
    bounds = [0, 3, 6, n_tiles]
    return [prepare] + [stage for lo, hi in zip(bounds[:-1], bounds[1:])
                        for stage in staged(lo, hi)]


def _rope_tables(seq):
    half = ROT_DIM // 2
    inv_freq = ROPE_THETA ** (-jnp.arange(0, ROT_DIM, 2, dtype=f32) / ROT_DIM)
    ang = jnp.arange(seq, dtype=f32)[:, None] * inv_freq[None, :]
    cos, sin = jnp.cos(ang), jnp.sin(ang)
    ones = jnp.ones((seq, HEAD_DIM - ROT_DIM), f32)
    zeros = jnp.zeros((seq, HEAD_DIM - ROT_DIM), f32)
    zh = jnp.zeros((seq, half), f32)
    cos_t = jnp.concatenate([cos, cos, ones], axis=1)
    up_t = jnp.concatenate([-sin, zh, zeros], axis=1)
    dn_t = jnp.concatenate([zh, sin, zeros], axis=1)
    tab = jnp.stack([cos_t, up_t, dn_t])
    return jnp.concatenate([tab, tab], axis=2)


def _attn_job(p, kv, sinks, rope, bsz, seq):
    nb = seq // ATT_BLOCK
    w = ATT_WIDTH
    prev = lambda step: step - jnp.minimum(step % nb, 1)
    return _SideJob(
        steps=bsz * nb,
        body=functools.partial(_attn_body, nb),
        inputs=(sinks, p, p, kv, kv, rope, rope),
        in_specs=(None,
                  ((ATT_BLOCK, w), lambda step: (step, 3)),
                  ((ATT_BLOCK, w), lambda step: (step, 4)),
                  ((ATT_BLOCK, 2 * KV_WIDTH), lambda step: (prev(step), 0)),
                  ((ATT_BLOCK, 2 * KV_WIDTH), lambda step: (step, 0)),
                  ((3, ATT_BLOCK, LANES), lambda step: (0, prev(step) % nb, 0)),
                  ((3, ATT_BLOCK, LANES), lambda step: (0, step % nb, 0))),
        out_shapes=(jax.ShapeDtypeStruct((bsz * seq, w), bf16),),
        out_specs=(((ATT_BLOCK, w), lambda step: (step, 0)),),
        scratch=())


def _merge_kernel(ya_ref, yb_ref, yc_ref, g_ref, x_ref, wa_ref, wb_ref, wc_ref, wo_ref,
                  nw_ref, *out_refs, last, chunk):
    merged_ref = out_refs[-1]
    ys = (ya_ref, yb_ref, yc_ref)
    ws = (wa_ref, wb_ref, wc_ref)
    for c in range(D_MODEL // chunk):
        cs = slice(c * chunk, (c + 1) * chunk)
        acc = None
        for k in range(N_BRANCH):
            gate = jax.nn.sigmoid(g_ref[:, k * D_MODEL + c * chunk:k * D_MODEL + (c + 1) * chunk]
                                  .astype(f32))
            term = gate * jnp.dot(ys[k][...], ws[k][:, cs], preferred_element_type=f32)
            acc = term if acc is None else acc + term
        merged_ref[:, cs] = acc.astype(bf16)
    x_new = x_ref[...] + jnp.dot(merged_ref[...], wo_ref[...], preferred_element_type=f32)
    ms = jnp.mean(x_new * x_new, axis=-1, keepdims=True)
    normed = x_new * lax.rsqrt(ms + EPS) * nw_ref[...]
    if last:
        out_refs[0][...] = normed
    else:
        out_refs[0][...] = x_new
        out_refs[1][...] = normed.astype(bf16)


def _merge_out(ya_tm, yb, yc, gates, x, wa, wb, wc, wo, next_norm_w, bsz, seq, last, tm=256):
    t = bsz * seq
    nt = seq // tm
    d = D_MODEL
    row = lambda b, n: (b * nt + n, 0)
    const = lambda b, n: (0, 0)
    once = pl.Buffered(1)
    wspec = lambda m: pl.BlockSpec(m.shape, const, pipeline_mode=once)
    if last:
        out_shape = [jax.ShapeDtypeStruct((t, d), f32)]
        out_specs = [pl.BlockSpec((tm, d), row)]
    else:
        out_shape = [jax.ShapeDtypeStruct((t, d), f32), jax.ShapeDtypeStruct((t, d), bf16)]
        out_specs = [pl.BlockSpec((tm, d), row), pl.BlockSpec((tm, d), row)]
    return pl.pallas_call(
        functools.partial(_merge_kernel, last=last, chunk=512),
        grid=(bsz, nt),
        in_specs=[pl.BlockSpec((tm, SSM_WIDTH), lambda b, n: (n, b)),
                  pl.BlockSpec((tm, SG_WIDTH), row),
                  pl.BlockSpec((tm, ATT_WIDTH), row),
                  pl.BlockSpec((tm, N_BRANCH * d), row),
                  pl.BlockSpec((tm, d), row),
                  wspec(wa), wspec(wb), wspec(wc), wspec(wo),
                  pl.BlockSpec((1, d), const)],
        out_specs=out_specs,
        out_shape=out_shape,
        scratch_shapes=[pltpu.VMEM((tm, d), bf16)],
        compiler_params=_cparams("parallel", "parallel"),
        name="merge_out",
    )(ya_tm, yb, yc, gates, x, wa, wb, wc, wo, next_norm_w.reshape(1, d))


def kernel(x, norm_w, w_in, ssm_a_re, ssm_a_im, ssm_log_dt, ssm_b_re, ssm_b_im, ssm_c_re, ssm_c_im, ssm_d, ssm_glu_w, ssm_glu_b, sg_ln_w, sg_ln_b, sg_w, sg_b, attn_sinks, w_branch_a, w_branch_b, w_branch_c, w_out, final_norm_w):
    bsz, seq, d = x.shape
    assert 2 * bsz == SUBLANES, "the S5 kernel packs two timesteps of all batches per sublane tile"
    depth = norm_w.shape[0]
    t = bsz * seq
    xf = x.reshape(t, d)
    rope = _rope_tables(seq)

    o_ub = 2 * SSM_WIDTH
    o_q = o_ub + 3 * SG_WIDTH
    o_k = o_q + ATT_WIDTH
    o_zc = o_k + 2 * KV_WIDTH
    o_g = o_zc + ATT_WIDTH

    tn = 1024
    gate_tn = 768
    s5_cols = [0, SSM_WIDTH]
    bc_cols = [o_ub, o_ub + tn, o_ub + 2 * tn, o_q, o_zc]
    gate_cols = [o_g + c * gate_tn for c in range(N_BRANCH * d // gate_tn)]

    b4s, lam2s, c2s = jax.vmap(_s5_params)(ssm_a_re, ssm_a_im, ssm_log_dt, ssm_b_re,
                                           ssm_b_im, ssm_c_re, ssm_c_im)
    layer_weights = (ssm_glu_w, w_branch_a, w_branch_b, w_branch_c, w_out)

    h, kv = _rmsnorm_kv(xf, norm_w[0], w_in[0, :, o_k:o_zc].astype(bf16))
    out = None
    for l in range(depth):
        p_bc, gw, wa, wb, wc, wo = _project(
            h, w_in, l, bc_cols, tn,
            side=_cast_job(layer_weights, l, steps=len(bc_cols) * (t // 1024)),
            activations=["gelu", "gelu", "silu", None, "silu"], name="proj_bc")
        if l > 0:
            kv = _project(h, w_in, l, [o_k], 2 * KV_WIDTH, name="proj_kv")
        uz, yb = _project(h, w_in, l, s5_cols, tn, time_major_batches=bsz,
                          side=_gmlp_job(p_bc, sg_ln_w[l], sg_ln_b[l], sg_w[l], sg_b[l]),
                          activations=[None, "silu"], name="proj_s5_gmlp")
        gates, yc = _project(h, w_in, l, gate_cols, gate_tn,
                             side=_attn_job(p_bc, kv, attn_sinks[l], rope, bsz, seq),
                             name="proj_gates_attn")
        ya = _s5_mixer(uz, b4s[l], lam2s[l], c2s[l], ssm_d[l], gw, ssm_glu_b[l], bsz)

        last = l == depth - 1
        next_w = final_norm_w if last else norm_w[l + 1]
        res = _merge_out(ya, yb, yc, gates, xf, wa, wb, wc, wo, next_w, bsz, seq, last)
        if last:
            out = res[0]
        else:
            xf, h = res
    return out.reshape(bsz, seq, d)
```

```python
import functools
from typing import Callable, NamedTuple

import jax
import jax.numpy as jnp
from jax import lax
from jax.experimental import pallas as pl
from jax.experimental.pallas import tpu as pltpu

f32 = jnp.float32
bf16 = jnp.bfloat16

D_MODEL = 2048
EPS = 1e-6
NEG_INF = -1e30

SSM_WIDTH = D_MODEL // 2
SSM_GROUP = 16
SSM_GROUPS = SSM_WIDTH // SSM_GROUP
SSM_STATE = 64

SG_WIDTH = D_MODEL // 2
SG_HEADS = 8
SG_CHUNK = 128

HEAD_DIM = 64
ATT_HEADS = D_MODEL // 128
ATT_KV_HEADS = ATT_HEADS // 8
ATT_WIDTH = ATT_HEADS * HEAD_DIM
KV_WIDTH = ATT_KV_HEADS * HEAD_DIM
ATT_BLOCK = 128
ROT_DIM = HEAD_DIM // 4
ROPE_THETA = 500000.0
N_BRANCH = 3

LANES = 128
SUBLANES = 8
MXU_DIM = 256
VMEM_LIMIT_BYTES = 48 * 1024 * 1024

NORM_ROWS = 512
PROJ_ROWS = 1024
PROJ_COLS = 1024
GATE_COLS = 768
GMLP_ROWS = 512
S5_STEPS = 128
CAST_BLOCKS = 32
MERGE_ROWS = 256
MERGE_CHUNK = 512

S5_GB = LANES // SSM_GROUP
S5_NGB = SSM_GROUPS // S5_GB
S5_HALF = S5_GB * SSM_STATE
S5_SW = 2 * S5_HALF


def _cparams(*sem):
    return pltpu.CompilerParams(dimension_semantics=sem, vmem_limit_bytes=VMEM_LIMIT_BYTES)


def _silu(x):
    return x * jax.nn.sigmoid(x)


def _rmsnorm_kernel(x_ref, w_ref, o_ref):
    x = x_ref[...]
    ms = jnp.mean(x * x, axis=-1, keepdims=True)
    o_ref[...] = (x * lax.rsqrt(ms + EPS) * w_ref[...]).astype(o_ref.dtype)


def _rmsnorm(x, w, out_dtype, tm=NORM_ROWS):
    t, d = x.shape
    return pl.pallas_call(
        _rmsnorm_kernel,
        grid=(t // tm,),
        in_specs=[pl.BlockSpec((tm, d), lambda i: (i, 0)),
                  pl.BlockSpec((1, d), lambda i: (0, 0))],
        out_specs=pl.BlockSpec((tm, d), lambda i: (i, 0)),
        out_shape=jax.ShapeDtypeStruct((t, d), out_dtype),
        compiler_params=_cparams("parallel"),
        name="rmsnorm",
    )(x, w.reshape(1, d))


class _SideJob(NamedTuple):
    steps: int
    body: Callable
    inputs: tuple
    in_specs: tuple
    out_shapes: tuple
    out_specs: tuple
    scratch: tuple


_ACTIVATIONS = {None: lambda r: r, "gelu": jax.nn.gelu, "silu": _silu}


def _proj_kernel(offs_ref, h_ref, w_ref, *refs, side_body, n_side_in, n_side_out, activations):
    del offs_ref
    side_in = refs[:n_side_in]
    o_ref = refs[n_side_in]
    side_out = refs[n_side_in + 1:n_side_in + 1 + n_side_out]
    wbf_ref = refs[n_side_in + 1 + n_side_out]
    side_scratch = refs[n_side_in + 2 + n_side_out:]

    @pl.when(pl.program_id(1) == 0)
    def _():
        wbf_ref[...] = w_ref[...].astype(bf16)

    def run(name):
        act = _ACTIVATIONS[name]
        tm, tn = o_ref.shape
        if side_body is None:
            o_ref[...] = act(jnp.dot(h_ref[...], wbf_ref[...],
                                     preferred_element_type=f32)).astype(o_ref.dtype)
            return
        step = pl.program_id(0) * pl.num_programs(1) + pl.program_id(1)
        parts = side_body(step, *side_in, *side_out, *side_scratch)
        pieces = [(rs, cs) for cs in range(0, tn, MXU_DIM) for rs in range(0, tm, MXU_DIM)]
        n_parts, n_gaps = len(parts), len(pieces) - 1
        done = 0
        for k, (rs, cs) in enumerate(pieces):
            o_ref[rs:rs + MXU_DIM, cs:cs + MXU_DIM] = act(jnp.dot(
                h_ref[rs:rs + MXU_DIM, :], wbf_ref[:, cs:cs + MXU_DIM],
                preferred_element_type=f32)).astype(o_ref.dtype)
            upto = min(n_parts, -(-(k + 1) * n_parts // n_gaps))
            for part in parts[done:upto]:
                part()
            done = upto

    blocks_of = {}
    for jb, name in enumerate(activations):
        blocks_of.setdefault(name, []).append(jb)
    if len(blocks_of) == 1:
        run(activations[0])
    else:
        j = pl.program_id(0)
        for name, blocks in blocks_of.items():
            pl.when(functools.reduce(jnp.logical_or, [j == jb for jb in blocks]))(
                functools.partial(run, name))


def _project(h, w_all, layer, col_offsets, tn, tm=PROJ_ROWS, time_major_batches=None, side=None,
             activations=None, name="proj"):
    t, k = h.shape
    nj = len(col_offsets)
    ni = t // tm
    assert all(off % LANES == 0 for off in col_offsets)
    assert side is None or side.steps == nj * ni
    offs = jnp.asarray([off // LANES for off in col_offsets], jnp.int32)
    if time_major_batches is None:
        out_shape = (t, nj * tn)
        out_map = lambda j, i, offs_ref: (i, j)
    else:
        seq = t // time_major_batches
        nt = seq // tm
        out_shape = (seq, nj * time_major_batches * tn)
        out_map = lambda j, i, offs_ref: (i % nt, j * time_major_batches + i // nt)

    def step_spec(spec):
        if spec is None:
            return pl.BlockSpec(memory_space=pltpu.SMEM)
        shape, index = spec
        return pl.BlockSpec(shape, lambda j, i, offs_ref: index(j * ni + i))

    side_in_specs = [] if side is None else [step_spec(sp) for sp in side.in_specs]
    side_out_specs = [] if side is None else [step_spec(sp) for sp in side.out_specs]
    side_inputs = () if side is None else side.inputs
    side_out_shapes = [] if side is None else list(side.out_shapes)
    side_scratch = [] if side is None else list(side.scratch)
    grid_spec = pltpu.PrefetchScalarGridSpec(
        num_scalar_prefetch=1,
        grid=(nj, ni),
        in_specs=[pl.BlockSpec((tm, k), lambda j, i, offs_ref: (i, 0)),
                  pl.BlockSpec((pl.Squeezed(), pl.Element(k), pl.Element(tn)),
                               lambda j, i, offs_ref: (layer, 0, offs_ref[j] * LANES))]
                 + side_in_specs,
        out_specs=[pl.BlockSpec((tm, tn), out_map)] + side_out_specs,
        scratch_shapes=[pltpu.VMEM((k, tn), bf16)] + side_scratch)
    res = pl.pallas_call(
        functools.partial(_proj_kernel, side_body=None if side is None else side.body,
                          n_side_in=len(side_in_specs), n_side_out=len(side_out_specs),
                          activations=tuple(activations or [None] * nj)),
        grid_spec=grid_spec,
        out_shape=[jax.ShapeDtypeStruct(out_shape, bf16)] + side_out_shapes,
        compiler_params=_cparams("arbitrary", "arbitrary"),
        name=name,
    )(offs, h, w_all, *side_inputs)
    return res[0] if side is None else res


def _tile_lanes(x, width):
    while width < LANES:
        x = x + pltpu.roll(x, width, 1)
        width *= 2
    return x


def _s5_kernel(u_ref, z_ref, b4_ref, lam2_ref, c2_ref, d_ref, gw_ref, gb_ref, o_ref,
               w2_ref, cm_ref, ubuf, bu2_ref, st2_ref, y_ref, carry_ref):
    i = pl.program_id(0)
    steps = u_ref.shape[0]
    bsz = SUBLANES // 2
    rows = steps * bsz
    w = SSM_WIDTH

    @pl.when(i == 0)
    def _():
        carry_ref[...] = jnp.zeros_like(carry_ref)
        ubuf[:, 0:SUBLANES, :] = jnp.zeros((S5_NGB, SUBLANES, LANES), f32)
        in_diag = (lax.broadcasted_iota(jnp.int32, (LANES, S5_HALF), 0) // SSM_GROUP
                   == lax.broadcasted_iota(jnp.int32, (LANES, S5_HALF), 1) // SSM_STATE)
        out_diag = (lax.broadcasted_iota(jnp.int32, (S5_HALF, LANES), 0) // SSM_STATE
                    == lax.broadcasted_iota(jnp.int32, (S5_HALF, LANES), 1) // SSM_GROUP)
        for g in range(S5_NGB):
            for k in range(4):
                x = _tile_lanes(b4_ref[k, g], SSM_STATE)
                x = jnp.concatenate([x] * (S5_HALF // LANES), axis=1)
                w2_ref[g, (k // 2) * LANES:(k // 2 + 1) * LANES,
                       (k % 2) * S5_HALF:(k % 2 + 1) * S5_HALF] = (
                           jnp.where(in_diag, x, 0.0).astype(bf16))
            for k in range(2):
                x = _tile_lanes(c2_ref[k, g], SSM_GROUP)
                cm_ref[g, k * S5_HALF:(k + 1) * S5_HALF, :] = (
                    jnp.where(out_diag, x, 0.0).astype(bf16))

    @pl.when(i > 0)
    def _():
        ubuf[:, 0:SUBLANES, :] = ubuf[:, rows:rows + SUBLANES, :]

    for g in range(S5_NGB):
        cs = slice(g * LANES, (g + 1) * LANES)
        for b in range(bsz):
            ubuf[g, pl.ds(SUBLANES + b, steps, stride=bsz), :] = (
                u_ref[:, b * w + g * LANES:b * w + (g + 1) * LANES].astype(f32))
        lhs = jnp.concatenate(
            [ubuf[g, SUBLANES:rows + SUBLANES, :].astype(bf16),
             ubuf[g, SUBLANES - bsz:rows + SUBLANES - bsz, :].astype(bf16)], axis=1)
        bu_ref = bu2_ref.at[g % 2]
        st_ref = st2_ref.at[g % 2]
        bu_ref[...] = jnp.dot(lhs, w2_ref[g], preferred_element_type=f32)
        l_re = lam2_ref[g, :, :S5_HALF]
        l_im = lam2_ref[g, :, S5_HALF:]

        def body(r, s, l_re=l_re, l_im=l_im):
            s_re, s_im = s
            off = pl.multiple_of(r * SUBLANES, SUBLANES)
            b = bu_ref[pl.ds(off, SUBLANES), :]
            n_re = l_re * s_re - l_im * s_im + b[:, :S5_HALF]
            n_im = l_re * s_im + l_im * s_re + b[:, S5_HALF:]
            st_ref[pl.ds(off, SUBLANES), :S5_HALF] = n_re
            st_ref[pl.ds(off, SUBLANES), S5_HALF:] = n_im
            return n_re, n_im

        s0 = (carry_ref[g, :, :S5_HALF], carry_ref[g, :, S5_HALF:])
        s_re, s_im = lax.fori_loop(0, rows // SUBLANES, body, s0, unroll=True)
        carry_ref[g, :, :S5_HALF] = s_re
        carry_ref[g, :, S5_HALF:] = s_im
        y_g = jnp.dot(st_ref[...].astype(bf16), cm_ref[g], preferred_element_type=f32)
        y_ref[g] = y_g + d_ref[:, cs] * ubuf[g, SUBLANES:rows + SUBLANES, :]

    y = jax.nn.gelu(jnp.concatenate([y_ref[g] for g in range(S5_NGB)], axis=1))
    gate = jnp.dot(y.astype(bf16), gw_ref[...], preferred_element_type=f32) + gb_ref[...]
    y = y * jax.nn.sigmoid(gate)
    for g in range(S5_NGB):
        y_ref[g] = y[:, g * LANES:(g + 1) * LANES]
    for b in range(bsz):
        for g in range(S5_NGB):
            cs = slice(b * w + g * LANES, b * w + (g + 1) * LANES)
            o_ref[:, cs] = (y_ref[g, pl.ds(b, steps, stride=bsz), :]
                            * z_ref[:, cs].astype(f32)).astype(o_ref.dtype)


def _s5_params(a_re, a_im, log_dt, b_re, b_im, c_re, c_im):
    dt = jnp.exp(log_dt)[:, None]
    mag = jnp.exp(a_re * dt)
    lb_re = mag * jnp.cos(a_im * dt)
    lb_im = mag * jnp.sin(a_im * dt)
    den = a_re * a_re + a_im * a_im
    k_re = ((lb_re - 1.0) * a_re + lb_im * a_im) / den
    k_im = (lb_im * a_re - (lb_re - 1.0) * a_im) / den
    bb_re = k_re[..., None] * b_re - k_im[..., None] * b_im
    bb_im = k_re[..., None] * b_im + k_im[..., None] * b_re
    t_re = lb_re[..., None] * bb_re - lb_im[..., None] * bb_im
    t_im = lb_re[..., None] * bb_im + lb_im[..., None] * bb_re

    def pad_lanes(m):
        return jnp.pad(m, ((0, 0), (0, 0), (0, LANES - m.shape[-1])))

    b4 = jnp.stack([pad_lanes(m.transpose(0, 2, 1).reshape(S5_NGB, LANES, SSM_STATE))
                    for m in (bb_re, bb_im, t_re, t_im)])
    c2 = jnp.stack([pad_lanes(m.transpose(0, 2, 1).reshape(S5_NGB, S5_HALF, SSM_GROUP))
                    for m in (c_re, -c_im)])
    l2_re = (lb_re * lb_re - lb_im * lb_im).reshape(S5_NGB, 1, S5_HALF)
    l2_im = (2.0 * lb_re * lb_im).reshape(S5_NGB, 1, S5_HALF)
    lam2 = jnp.broadcast_to(jnp.concatenate([l2_re, l2_im], axis=2),
                            (S5_NGB, SUBLANES, S5_SW))
    return b4, lam2, c2


def _s5_mixer(uz, b4, lam2, c2, d, glu_w, glu_b, bsz, steps=S5_STEPS):
    seq = uz.shape[0]
    w = SSM_WIDTH
    bw = bsz * w
    rows = steps * bsz
    const4 = lambda i: (0, 0, 0, 0)
    const3 = lambda i: (0, 0, 0)
    const2 = lambda i: (0, 0)
    return pl.pallas_call(
        _s5_kernel,
        grid=(seq // steps,),
        in_specs=[pl.BlockSpec((steps, bw), lambda i: (i, 0)),
                  pl.BlockSpec((steps, bw), lambda i: (i, 1)),
                  pl.BlockSpec(b4.shape, const4),
                  pl.BlockSpec(lam2.shape, const3),
                  pl.BlockSpec(c2.shape, const4),
                  pl.BlockSpec((1, w), const2),
                  pl.BlockSpec((w, w), const2),
                  pl.BlockSpec((1, w), const2)],
        out_specs=pl.BlockSpec((steps, bw), lambda i: (i, 0)),
        out_shape=jax.ShapeDtypeStruct((seq, bw), bf16),
        scratch_shapes=[pltpu.VMEM((S5_NGB, 2 * LANES, S5_SW), bf16),
                        pltpu.VMEM((S5_NGB, S5_SW, LANES), bf16),
                        pltpu.VMEM((S5_NGB, rows + SUBLANES, LANES), f32),
                        pltpu.VMEM((2, rows, S5_SW), f32),
                        pltpu.VMEM((2, rows, S5_SW), f32),
                        pltpu.VMEM((S5_NGB, rows, LANES), f32),
                        pltpu.VMEM((S5_NGB, SUBLANES, S5_SW), f32)],
        compiler_params=_cparams("arbitrary"),
        name="s5_mixer",
    )(uz, uz, b4, lam2, c2, d.reshape(1, w), glu_w, glu_b.reshape(1, w))


def _cast_body(n_arrays, step, *refs):
    del step

    def cast():
        for src, dst in zip(refs[:n_arrays], refs[n_arrays:]):
            dst[...] = src[...].astype(dst.dtype)

    return [cast]


def _cast_job(arrays, layer, steps, n_blocks=CAST_BLOCKS):
    block = lambda step: (jnp.minimum(step, n_blocks - 1), 0)
    specs, shapes = [], []
    for m in arrays:
        rows = m.shape[1] // n_blocks
        assert rows * n_blocks == m.shape[1] and rows % (2 * SUBLANES) == 0
        specs.append(((pl.Squeezed(), rows, m.shape[2]),
                      lambda step: (layer, jnp.minimum(step, n_blocks - 1), 0)))
        shapes.append(((rows, m.shape[2]), jax.ShapeDtypeStruct(m.shape[1:], bf16)))
    return _SideJob(
        steps=steps,
        body=functools.partial(_cast_body, len(arrays)),
        inputs=tuple(arrays),
        in_specs=tuple(specs),
        out_shapes=tuple(sh for _, sh in shapes),
        out_specs=tuple((blk, block) for blk, _ in shapes),
        scratch=())


def _gmlp_body(step, u_ref, v_ref, z_ref, lnw_ref, lnb_ref, ws_ref, bias_ref, o_ref,
               vn_ref, wm_ref):
    del step
    rows = u_ref.shape[0]

    def norm_part(c):
        rs = slice(c * SG_CHUNK, (c + 1) * SG_CHUNK)
        if c == 0:
            t_idx = lax.broadcasted_iota(jnp.int32, (SG_CHUNK, SG_CHUNK), 0)
            s_idx = lax.broadcasted_iota(jnp.int32, (SG_CHUNK, SG_CHUNK), 1)
            for h in range(SG_HEADS):
                wm_ref[h] = jnp.where(s_idx <= t_idx, ws_ref[h], 0.0).astype(bf16)
        v = v_ref[rs, :].astype(f32)
        mu = jnp.mean(v, axis=-1, keepdims=True)
        vc = v - mu
        var = jnp.mean(vc * vc, axis=-1, keepdims=True)
        vn_ref[rs, :] = (vc * lax.rsqrt(var + EPS) * lnw_ref[...] + lnb_ref[...]).astype(bf16)

    def mix_part(c, h):
        rs = slice(c * SG_CHUNK, (c + 1) * SG_CHUNK)
        cs = slice(h * LANES, (h + 1) * LANES)
        mixed = jnp.dot(wm_ref[h], vn_ref[rs, cs], preferred_element_type=f32)
        mixed = mixed + bias_ref[:, cs]
        o_ref[rs, cs] = (u_ref[rs, cs].astype(f32) * mixed
                         * z_ref[rs, cs].astype(f32)).astype(o_ref.dtype)

    parts = []
    for c in range(rows // SG_CHUNK):
        parts.append(functools.partial(norm_part, c))
        parts.extend(functools.partial(mix_part, c, h) for h in range(SG_HEADS))
    return parts


def _gmlp_job(p, ln_w, ln_b, w_s, b_s, rows=GMLP_ROWS):
    t = p.shape[0]
    w = SG_WIDTH
    bias = jnp.repeat(b_s.T, LANES, axis=1)
    const2 = lambda step: (0, 0)
    return _SideJob(
        steps=t // rows,
        body=_gmlp_body,
        inputs=(p, p, p, ln_w.reshape(1, w), ln_b.reshape(1, w), w_s, bias),
        in_specs=(((rows, w), lambda step: (step, 0)),
                  ((rows, w), lambda step: (step, 1)),
                  ((rows, w), lambda step: (step, 2)),
                  ((1, w), const2),
                  ((1, w), const2),
                  (w_s.shape, lambda step: (0, 0, 0)),
                  ((SG_CHUNK, w), const2)),
        out_shapes=(jax.ShapeDtypeStruct((t, w), bf16),),
        out_specs=(((rows, w), lambda step: (step, 0)),),
        scratch=(pltpu.VMEM((rows, w), bf16),
                 pltpu.VMEM((SG_HEADS, SG_CHUNK, SG_CHUNK), bf16)))


def _rope(x, cos, sin_up, sin_dn):
    half = ROT_DIM // 2
    return (x * cos + pltpu.roll(x, LANES - half, 1) * sin_up
            + pltpu.roll(x, half, 1) * sin_dn)


def _attn_body(nb, step, sink_ref, q_ref, z_ref, kvp_ref, kvc_ref, rp_ref, rc_ref, o_ref):
    n = step % nb
    blk = ATT_BLOCK
    group = ATT_HEADS // ATT_KV_HEADS
    pre = {}

    def prepare():
        cos_c, up_c, dn_c = rc_ref[0], rc_ref[1], rc_ref[2]
        cos_p, up_p, dn_p = rp_ref[0], rp_ref[1], rp_ref[2]
        k_all = jnp.concatenate([
            _rope(kvp_ref[:, :LANES].astype(f32), cos_p, up_p, dn_p),
            _rope(kvc_ref[:, :LANES].astype(f32), cos_c, up_c, dn_c)], axis=0)
        v_all = jnp.concatenate([kvp_ref[:, LANES:].astype(f32),
                                 kvc_ref[:, LANES:].astype(f32)], axis=0)
        lane = lax.broadcasted_iota(jnp.int32, (2 * blk, LANES), 1)
        low = lane < HEAD_DIM
        k_sw = pltpu.roll(k_all, HEAD_DIM, 1)
        v_sw = pltpu.roll(v_all, HEAD_DIM, 1)
        pre["k"] = [[jnp.where(low, k_all, 0.0).astype(bf16),
                     jnp.where(low, 0.0, k_sw).astype(bf16)],
                    [jnp.where(low, k_sw, 0.0).astype(bf16),
                     jnp.where(low, 0.0, k_all).astype(bf16)]]
        pre["v"] = [[jnp.where(low, v_all, 0.0).astype(bf16),
                     jnp.where(low, 0.0, v_sw).astype(bf16)],
                    [jnp.where(low, v_sw, 0.0).astype(bf16),
                     jnp.where(low, 0.0, v_all).astype(bf16)]]
        qi = lax.broadcasted_iota(jnp.int32, (blk, blk), 0)
        kj = lax.broadcasted_iota(jnp.int32, (blk, blk), 1)
        pre["in_cur"] = kj <= qi
        pre["prev_bias"] = jnp.where(pre["in_cur"] | (n > 0), 0.0, NEG_INF)
        scale = HEAD_DIM ** -0.5
        pre["rope_q"] = (cos_c * scale, up_c * scale, dn_c * scale)

    def staged(tile_lo, tile_hi):
        heads = range(2 * tile_lo, 2 * tile_hi)
        tiles = {j: slice(j * LANES, (j + 1) * LANES) for j in range(tile_lo, tile_hi)}
        st = {}

        def scores():
            qs = {j: _rope(q_ref[:, cs].astype(f32), *pre["rope_q"]).astype(bf16)
                  for j, cs in tiles.items()}
            s2 = {hd: lax.dot_general(qs[hd // 2], pre["k"][hd // group][hd % 2],
                                      (((1,), (1,)), ((), ())), preferred_element_type=f32)
                  for hd in heads}
            st["s"] = {hd: jnp.where(pre["in_cur"], s2[hd][:, blk:], s2[hd][:, :blk])
                       + pre["prev_bias"] for hd in heads}

        def row_max():
            st["m"] = {hd: jnp.maximum(jnp.max(st["s"][hd], axis=-1, keepdims=True), sink_ref[hd])
                       for hd in heads}

        def exponent():
            st["p"] = {hd: jnp.exp(st["s"][hd] - st["m"][hd]) for hd in heads}
            st["den"] = {hd: jnp.sum(st["p"][hd], axis=-1, keepdims=True)
                         + jnp.exp(sink_ref[hd] - st["m"][hd]) for hd in heads}

        def values():
            p, in_cur = st["p"], pre["in_cur"]
            p2 = {hd: jnp.concatenate([jnp.where(in_cur, 0.0, p[hd]),
                                       jnp.where(in_cur, p[hd], 0.0)], axis=1).astype(bf16)
                  for hd in heads}
            st["pv"] = {hd: jnp.dot(p2[hd], pre["v"][hd // group][hd % 2],
                                    preferred_element_type=f32) * (1.0 / st["den"][hd])
                        for hd in heads}

        def output():
            pv = st["pv"]
            for j, cs in tiles.items():
                o_ref[:, cs] = ((pv[2 * j] + pv[2 * j + 1])
                                * z_ref[:, cs].astype(f32)).astype(o_ref.dtype)

        return [scores, row_max, exponent, values, output]

    n_tiles = ATT_HEADS // 2
    bounds = [0, 3, 6, n_tiles]
    return [prepare] + [stage for lo, hi in zip(bounds[:-1], bounds[1:])
                        for stage in staged(lo, hi)]


def _rope_tables(seq):
    half = ROT_DIM // 2
    inv_freq = ROPE_THETA ** (-jnp.arange(0, ROT_DIM, 2, dtype=f32) / ROT_DIM)
    ang = jnp.arange(seq, dtype=f32)[:, None] * inv_freq[None, :]
    cos, sin = jnp.cos(ang), jnp.sin(ang)
    ones = jnp.ones((seq, HEAD_DIM - ROT_DIM), f32)
    zeros = jnp.zeros((seq, HEAD_DIM - ROT_DIM), f32)
    zh = jnp.zeros((seq, half), f32)
    cos_t = jnp.concatenate([cos, cos, ones], axis=1)
    up_t = jnp.concatenate([-sin, zh, zeros], axis=1)
    dn_t = jnp.concatenate([zh, sin, zeros], axis=1)
    tab = jnp.stack([cos_t, up_t, dn_t])
    return jnp.concatenate([tab, tab], axis=2)


def _attn_job(p, kv, sinks, rope, bsz, seq):
    nb = seq // ATT_BLOCK
    w = ATT_WIDTH
    prev = lambda step: step - jnp.minimum(step % nb, 1)
    return _SideJob(
        steps=bsz * nb,
        body=functools.partial(_attn_body, nb),
        inputs=(sinks, p, p, kv, kv, rope, rope),
        in_specs=(None,
                  ((ATT_BLOCK, w), lambda step: (step, 3)),
                  ((ATT_BLOCK, w), lambda step: (step, 4)),
                  ((ATT_BLOCK, 2 * KV_WIDTH), lambda step: (prev(step), 0)),
                  ((ATT_BLOCK, 2 * KV_WIDTH), lambda step: (step, 0)),
                  ((3, ATT_BLOCK, LANES), lambda step: (0, prev(step) % nb, 0)),
                  ((3, ATT_BLOCK, LANES), lambda step: (0, step % nb, 0))),
        out_shapes=(jax.ShapeDtypeStruct((bsz * seq, w), bf16),),
        out_specs=(((ATT_BLOCK, w), lambda step: (step, 0)),),
        scratch=())


def _merge_kernel(ya_ref, yb_ref, yc_ref, g_ref, x_ref, wa_ref, wb_ref, wc_ref, wo_ref,
                  nw_ref, *out_refs, last, chunk):
    merged_ref = out_refs[-1]
    ys = (ya_ref, yb_ref, yc_ref)
    ws = (wa_ref, wb_ref, wc_ref)
    for c in range(D_MODEL // chunk):
        cs = slice(c * chunk, (c + 1) * chunk)
        acc = None
        for k in range(N_BRANCH):
            gate = jax.nn.sigmoid(g_ref[:, k * D_MODEL + c * chunk:k * D_MODEL + (c + 1) * chunk]
                                  .astype(f32))
            term = gate * jnp.dot(ys[k][...], ws[k][:, cs], preferred_element_type=f32)
            acc = term if acc is None else acc + term
        merged_ref[:, cs] = acc.astype(bf16)
    x_new = x_ref[...] + jnp.dot(merged_ref[...], wo_ref[...], preferred_element_type=f32)
    ms = jnp.mean(x_new * x_new, axis=-1, keepdims=True)
    normed = x_new * lax.rsqrt(ms + EPS) * nw_ref[...]
    if last:
        out_refs[0][...] = normed
    else:
        out_refs[0][...] = x_new
        out_refs[1][...] = normed.astype(bf16)


def _merge_out(ya_tm, yb, yc, gates, x, wa, wb, wc, wo, next_norm_w, bsz, seq, last,
               tm=MERGE_ROWS):
    t = bsz * seq
    nt = seq // tm
    d = D_MODEL
    row = lambda b, n: (b * nt + n, 0)
    const = lambda b, n: (0, 0)
    once = pl.Buffered(1)
    wspec = lambda m: pl.BlockSpec(m.shape, const, pipeline_mode=once)
    if last:
        out_shape = [jax.ShapeDtypeStruct((t, d), f32)]
        out_specs = [pl.BlockSpec((tm, d), row)]
    else:
        out_shape = [jax.ShapeDtypeStruct((t, d), f32), jax.ShapeDtypeStruct((t, d), bf16)]
        out_specs = [pl.BlockSpec((tm, d), row), pl.BlockSpec((tm, d), row)]
    return pl.pallas_call(
        functools.partial(_merge_kernel, last=last, chunk=MERGE_CHUNK),
        grid=(bsz, nt),
        in_specs=[pl.BlockSpec((tm, SSM_WIDTH), lambda b, n: (n, b)),
                  pl.BlockSpec((tm, SG_WIDTH), row),
                  pl.BlockSpec((tm, ATT_WIDTH), row),
                  pl.BlockSpec((tm, N_BRANCH * d), row),
                  pl.BlockSpec((tm, d), row),
                  wspec(wa), wspec(wb), wspec(wc), wspec(wo),
                  pl.BlockSpec((1, d), const)],
        out_specs=out_specs,
        out_shape=out_shape,
        scratch_shapes=[pltpu.VMEM((tm, d), bf16)],
        compiler_params=_cparams("parallel", "parallel"),
        name="merge_out",
    )(ya_tm, yb, yc, gates, x, wa, wb, wc, wo, next_norm_w.reshape(1, d))


def kernel(x, norm_w, w_in, ssm_a_re, ssm_a_im, ssm_log_dt, ssm_b_re, ssm_b_im, ssm_c_re, ssm_c_im, ssm_d, ssm_glu_w, ssm_glu_b, sg_ln_w, sg_ln_b, sg_w, sg_b, attn_sinks, w_branch_a, w_branch_b, w_branch_c, w_out, final_norm_w):
    bsz, seq, d = x.shape
    assert 2 * bsz == SUBLANES, "the S5 kernel packs two timesteps of all batches per sublane tile"
    depth = norm_w.shape[0]
    t = bsz * seq
    xf = x.reshape(t, d)
    rope = _rope_tables(seq)

    o_ub = 2 * SSM_WIDTH
    o_q = o_ub + 3 * SG_WIDTH
    o_k = o_q + ATT_WIDTH
    o_zc = o_k + 2 * KV_WIDTH
    o_g = o_zc + ATT_WIDTH

    tn = PROJ_COLS
    gate_tn = GATE_COLS
    s5_cols = [0, SSM_WIDTH]
    bc_cols = [o_ub, o_ub + tn, o_ub + 2 * tn, o_q, o_zc]
    gate_cols = [o_g + c * gate_tn for c in range(N_BRANCH * d // gate_tn)]

    b4s, lam2s, c2s = jax.vmap(_s5_params)(ssm_a_re, ssm_a_im, ssm_log_dt, ssm_b_re,
                                           ssm_b_im, ssm_c_re, ssm_c_im)
    layer_weights = (ssm_glu_w, w_branch_a, w_branch_b, w_branch_c, w_out)

    h = _rmsnorm(xf, norm_w[0], bf16)
    out = None
    for l in range(depth):
        p_bc, gw, wa, wb, wc, wo = _project(
            h, w_in, l, bc_cols, tn,
            side=_cast_job(layer_weights, l, steps=len(bc_cols) * (t // PROJ_ROWS)),
            activations=["gelu", "gelu", "silu", None, "silu"], name="proj_bc")
        kv = _project(h, w_in, l, [o_k], 2 * KV_WIDTH, name="proj_kv")
        uz, yb = _project(h, w_in, l, s5_cols, tn, time_major_batches=bsz,
                          side=_gmlp_job(p_bc, sg_ln_w[l], sg_ln_b[l], sg_w[l], sg_b[l]),
                          activations=[None, "silu"], name="proj_s5_gmlp")
        gates, yc = _project(h, w_in, l, gate_cols, gate_tn,
                             side=_attn_job(p_bc, kv, attn_sinks[l], rope, bsz, seq),
                             name="proj_gates_attn")
        ya = _s5_mixer(uz, b4s[l], lam2s[l], c2s[l], ssm_d[l], gw, ssm_glu_b[l], bsz)

        last = l == depth - 1
        next_w = final_norm_w if last else norm_w[l + 1]
        res = _merge_out(ya, yb, yc, gates, xf, wa, wb, wc, wo, next_w, bsz, seq, last)
        if last:
            out = res[0]
        else:
            xf, h = res
    return out.reshape(bsz, seq, d)
```

```python
import functools
from typing import Callable, NamedTuple

import jax
import jax.numpy as jnp
from jax import lax
from jax.experimental import pallas as pl
from jax.experimental.pallas import tpu as pltpu

f32 = jnp.float32
bf16 = jnp.bfloat16

D_MODEL = 2048
EPS = 1e-6
NEG_INF = -1e30

SSM_WIDTH = D_MODEL // 2
SSM_GROUP = 16
SSM_GROUPS = SSM_WIDTH // SSM_GROUP
SSM_STATE = 64

SG_WIDTH = D_MODEL // 2
SG_HEADS = 8
SG_CHUNK = 128

HEAD_DIM = 64
ATT_HEADS = D_MODEL // 128
ATT_KV_HEADS = ATT_HEADS // 8
ATT_WIDTH = ATT_HEADS * HEAD_DIM
KV_WIDTH = ATT_KV_HEADS * HEAD_DIM
ATT_BLOCK = 128
ROT_DIM = HEAD_DIM // 4
ROPE_THETA = 500000.0
N_BRANCH = 3

LANES = 128
SUBLANES = 8
MXU_DIM = 256
VMEM_LIMIT_BYTES = 48 * 1024 * 1024

NORM_ROWS = 512
PROJ_ROWS = 1024
PROJ_COLS = 1024
GATE_COLS = 768
GMLP_ROWS = 512
S5_STEPS = 128
CAST_BLOCKS = 32
MERGE_ROWS = 256
MERGE_CHUNK = 512

S5_GB = LANES // SSM_GROUP
S5_NGB = SSM_GROUPS // S5_GB
S5_HALF = S5_GB * SSM_STATE
S5_SW = 2 * S5_HALF


def _cparams(*sem):
    return pltpu.CompilerParams(dimension_semantics=sem, vmem_limit_bytes=VMEM_LIMIT_BYTES)


def _silu(x):
    return x * jax.nn.sigmoid(x)


def _rmsnorm_kernel(x_ref, w_ref, o_ref):
    x = x_ref[...]
    ms = jnp.mean(x * x, axis=-1, keepdims=True)
    o_ref[...] = (x * lax.rsqrt(ms + EPS) * w_ref[...]).astype(o_ref.dtype)


def _rmsnorm(x, w, out_dtype, tm=NORM_ROWS):
    t, d = x.shape
    return pl.pallas_call(
        _rmsnorm_kernel,
        grid=(t // tm,),
        in_specs=[pl.BlockSpec((tm, d), lambda i: (i, 0)),
                  pl.BlockSpec((1, d), lambda i: (0, 0))],
        out_specs=pl.BlockSpec((tm, d), lambda i: (i, 0)),
        out_shape=jax.ShapeDtypeStruct((t, d), out_dtype),
        compiler_params=_cparams("parallel"),
        name="rmsnorm",
    )(x, w.reshape(1, d))


class _SideJob(NamedTuple):
    steps: int
    body: Callable
    inputs: tuple
    in_specs: tuple
    out_shapes: tuple
    out_specs: tuple
    scratch: tuple


_ACTIVATIONS = {None: lambda r: r, "gelu": jax.nn.gelu, "silu": _silu}


def _proj_kernel(offs_ref, h_ref, w_ref, *refs, side_body, n_side_in, n_side_out, activations):
    del offs_ref
    side_in = refs[:n_side_in]
    o_ref = refs[n_side_in]
    side_out = refs[n_side_in + 1:n_side_in + 1 + n_side_out]
    wbf_ref = refs[n_side_in + 1 + n_side_out]
    stage_ref = refs[n_side_in + 2 + n_side_out]
    side_scratch = refs[n_side_in + 3 + n_side_out:]

    @pl.when(pl.program_id(1) == 0)
    def _():
        wbf_ref[...] = w_ref[...].astype(bf16)

    def run(name):
        act = _ACTIVATIONS[name]
        tm, tn = o_ref.shape
        if side_body is None:
            o_ref[...] = act(jnp.dot(h_ref[...], wbf_ref[...],
                                     preferred_element_type=f32)).astype(o_ref.dtype)
            return
        step = pl.program_id(0) * pl.num_programs(1) + pl.program_id(1)
        parts = side_body(step, *side_in, *side_out, *side_scratch)
        pieces = [(rs, cs) for cs in range(0, tn, MXU_DIM) for rs in range(0, tm, MXU_DIM)]
        n_parts, n_gaps = len(parts), len(pieces) - 1
        done = 0
        behind = None
        for k, (rs, cs) in enumerate(pieces):
            acc = jnp.dot(h_ref[rs:rs + MXU_DIM, :], wbf_ref[:, cs:cs + MXU_DIM],
                          preferred_element_type=f32)
            if name is None:
                o_ref[rs:rs + MXU_DIM, cs:cs + MXU_DIM] = acc.astype(o_ref.dtype)
            else:
                stage_ref[k % 2] = acc
                if behind is not None:
                    prs, pcs = behind
                    o_ref[prs:prs + MXU_DIM, pcs:pcs + MXU_DIM] = act(
                        stage_ref[(k - 1) % 2]).astype(o_ref.dtype)
                behind = (rs, cs)
            upto = min(n_parts, -(-(k + 1) * n_parts // n_gaps))
            for part in parts[done:upto]:
                part()
            done = upto
        if behind is not None:
            prs, pcs = behind
            o_ref[prs:prs + MXU_DIM, pcs:pcs + MXU_DIM] = act(
                stage_ref[(len(pieces) - 1) % 2]).astype(o_ref.dtype)

    blocks_of = {}
    for jb, name in enumerate(activations):
        blocks_of.setdefault(name, []).append(jb)
    if len(blocks_of) == 1:
        run(activations[0])
    else:
        j = pl.program_id(0)
        for name, blocks in blocks_of.items():
            pl.when(functools.reduce(jnp.logical_or, [j == jb for jb in blocks]))(
                functools.partial(run, name))


def _project(h, w_all, layer, col_offsets, tn, tm=PROJ_ROWS, time_major_batches=None, side=None,
             activations=None, name="proj"):
    t, k = h.shape
    nj = len(col_offsets)
    ni = t // tm
    assert all(off % LANES == 0 for off in col_offsets)
    assert side is None or side.steps == nj * ni
    offs = jnp.asarray([off // LANES for off in col_offsets], jnp.int32)
    if time_major_batches is None:
        out_shape = (t, nj * tn)
        out_map = lambda j, i, offs_ref: (i, j)
    else:
        seq = t // time_major_batches
        nt = seq // tm
        out_shape = (seq, nj * time_major_batches * tn)
        out_map = lambda j, i, offs_ref: (i % nt, j * time_major_batches + i // nt)

    def step_spec(spec):
        if spec is None:
            return pl.BlockSpec(memory_space=pltpu.SMEM)
        shape, index = spec
        return pl.BlockSpec(shape, lambda j, i, offs_ref: index(j * ni + i))

    side_in_specs = [] if side is None else [step_spec(sp) for sp in side.in_specs]
    side_out_specs = [] if side is None else [step_spec(sp) for sp in side.out_specs]
    side_inputs = () if side is None else side.inputs
    side_out_shapes = [] if side is None else list(side.out_shapes)
    side_scratch = [] if side is None else list(side.scratch)
    grid_spec = pltpu.PrefetchScalarGridSpec(
        num_scalar_prefetch=1,
        grid=(nj, ni),
        in_specs=[pl.BlockSpec((tm, k), lambda j, i, offs_ref: (i, 0)),
                  pl.BlockSpec((pl.Squeezed(), pl.Element(k), pl.Element(tn)),
                               lambda j, i, offs_ref: (layer, 0, offs_ref[j] * LANES))]
                 + side_in_specs,
        out_specs=[pl.BlockSpec((tm, tn), out_map)] + side_out_specs,
        scratch_shapes=[pltpu.VMEM((k, tn), bf16),
                        pltpu.VMEM((2, MXU_DIM, MXU_DIM), f32)] + side_scratch)
    res = pl.pallas_call(
        functools.partial(_proj_kernel, side_body=None if side is None else side.body,
                          n_side_in=len(side_in_specs), n_side_out=len(side_out_specs),
                          activations=tuple(activations or [None] * nj)),
        grid_spec=grid_spec,
        out_shape=[jax.ShapeDtypeStruct(out_shape, bf16)] + side_out_shapes,
        compiler_params=_cparams("arbitrary", "arbitrary"),
        name=name,
    )(offs, h, w_all, *side_inputs)
    return res[0] if side is None else res


def _tile_lanes(x, width):
    while width < LANES:
        x = x + pltpu.roll(x, width, 1)
        width *= 2
    return x


def _s5_kernel(u_ref, z_ref, b4_ref, lam2_ref, c2_ref, d_ref, gw_ref, gb_ref, o_ref,
               w2_ref, cm_ref, ubuf, bu2_ref, st2_ref, y_ref, carry_ref):
    i = pl.program_id(0)
    steps = u_ref.shape[0]
    bsz = SUBLANES // 2
    rows = steps * bsz
    w = SSM_WIDTH

    @pl.when(i == 0)
    def _():
        carry_ref[...] = jnp.zeros_like(carry_ref)
        ubuf[:, 0:SUBLANES, :] = jnp.zeros((S5_NGB, SUBLANES, LANES), f32)
        in_diag = (lax.broadcasted_iota(jnp.int32, (LANES, S5_HALF), 0) // SSM_GROUP
                   == lax.broadcasted_iota(jnp.int32, (LANES, S5_HALF), 1) // SSM_STATE)
        out_diag = (lax.broadcasted_iota(jnp.int32, (S5_HALF, LANES), 0) // SSM_STATE
                    == lax.broadcasted_iota(jnp.int32, (S5_HALF, LANES), 1) // SSM_GROUP)
        for g in range(S5_NGB):
            for k in range(4):
                x = _tile_lanes(b4_ref[k, g], SSM_STATE)
                x = jnp.concatenate([x] * (S5_HALF // LANES), axis=1)
                w2_ref[g, (k // 2) * LANES:(k // 2 + 1) * LANES,
                       (k % 2) * S5_HALF:(k % 2 + 1) * S5_HALF] = (
                           jnp.where(in_diag, x, 0.0).astype(bf16))
            for k in range(2):
                x = _tile_lanes(c2_ref[k, g], SSM_GROUP)
                cm_ref[g, k * S5_HALF:(k + 1) * S5_HALF, :] = (
                    jnp.where(out_diag, x, 0.0).astype(bf16))

    @pl.when(i > 0)
    def _():
        ubuf[:, 0:SUBLANES, :] = ubuf[:, rows:rows + SUBLANES, :]

    for g in range(S5_NGB):
        cs = slice(g * LANES, (g + 1) * LANES)
        for b in range(bsz):
            ubuf[g, pl.ds(SUBLANES + b, steps, stride=bsz), :] = (
                u_ref[:, b * w + g * LANES:b * w + (g + 1) * LANES].astype(f32))
        lhs = jnp.concatenate(
            [ubuf[g, SUBLANES:rows + SUBLANES, :].astype(bf16),
             ubuf[g, SUBLANES - bsz:rows + SUBLANES - bsz, :].astype(bf16)], axis=1)
        bu_ref = bu2_ref.at[g % 2]
        st_ref = st2_ref.at[g % 2]
        bu_ref[...] = jnp.dot(lhs, w2_ref[g], preferred_element_type=f32)
        l_re = lam2_ref[g, :, :S5_HALF]
        l_im = lam2_ref[g, :, S5_HALF:]

        def body(r, s, l_re=l_re, l_im=l_im):
            s_re, s_im = s
            off = pl.multiple_of(r * SUBLANES, SUBLANES)
            b = bu_ref[pl.ds(off, SUBLANES), :]
            n_re = l_re * s_re - l_im * s_im + b[:, :S5_HALF]
            n_im = l_re * s_im + l_im * s_re + b[:, S5_HALF:]
            st_ref[pl.ds(off, SUBLANES), :S5_HALF] = n_re
            st_ref[pl.ds(off, SUBLANES), S5_HALF:] = n_im
            return n_re, n_im

        s0 = (carry_ref[g, :, :S5_HALF], carry_ref[g, :, S5_HALF:])
        s_re, s_im = lax.fori_loop(0, rows // SUBLANES, body, s0, unroll=True)
        carry_ref[g, :, :S5_HALF] = s_re
        carry_ref[g, :, S5_HALF:] = s_im
        y_g = jnp.dot(st_ref[...].astype(bf16), cm_ref[g], preferred_element_type=f32)
        y_ref[g] = y_g + d_ref[:, cs] * ubuf[g, SUBLANES:rows + SUBLANES, :]

    y = jax.nn.gelu(jnp.concatenate([y_ref[g] for g in range(S5_NGB)], axis=1))
    gate = jnp.dot(y.astype(bf16), gw_ref[...], preferred_element_type=f32) + gb_ref[...]
    y = y * jax.nn.sigmoid(gate)
    for g in range(S5_NGB):
        y_ref[g] = y[:, g * LANES:(g + 1) * LANES]
    for b in range(bsz):
        for g in range(S5_NGB):
            cs = slice(b * w + g * LANES, b * w + (g + 1) * LANES)
            o_ref[:, cs] = (y_ref[g, pl.ds(b, steps, stride=bsz), :]
                            * z_ref[:, cs].astype(f32)).astype(o_ref.dtype)


def _s5_params(a_re, a_im, log_dt, b_re, b_im, c_re, c_im):
    dt = jnp.exp(log_dt)[:, None]
    mag = jnp.exp(a_re * dt)
    lb_re = mag * jnp.cos(a_im * dt)
    lb_im = mag * jnp.sin(a_im * dt)
    den = a_re * a_re + a_im * a_im
    k_re = ((lb_re - 1.0) * a_re + lb_im * a_im) / den
    k_im = (lb_im * a_re - (lb_re - 1.0) * a_im) / den
    bb_re = k_re[..., None] * b_re - k_im[..., None] * b_im
    bb_im = k_re[..., None] * b_im + k_im[..., None] * b_re
    t_re = lb_re[..., None] * bb_re - lb_im[..., None] * bb_im
    t_im = lb_re[..., None] * bb_im + lb_im[..., None] * bb_re

    def pad_lanes(m):
        return jnp.pad(m, ((0, 0), (0, 0), (0, LANES - m.shape[-1])))

    b4 = jnp.stack([pad_lanes(m.transpose(0, 2, 1).reshape(S5_NGB, LANES, SSM_STATE))
                    for m in (bb_re, bb_im, t_re, t_im)])
    c2 = jnp.stack([pad_lanes(m.transpose(0, 2, 1).reshape(S5_NGB, S5_HALF, SSM_GROUP))
                    for m in (c_re, -c_im)])
    l2_re = (lb_re * lb_re - lb_im * lb_im).reshape(S5_NGB, 1, S5_HALF)
    l2_im = (2.0 * lb_re * lb_im).reshape(S5_NGB, 1, S5_HALF)
    lam2 = jnp.broadcast_to(jnp.concatenate([l2_re, l2_im], axis=2),
                            (S5_NGB, SUBLANES, S5_SW))
    return b4, lam2, c2


def _s5_mixer(uz, b4, lam2, c2, d, glu_w, glu_b, bsz, steps=S5_STEPS):
    seq = uz.shape[0]
    w = SSM_WIDTH
    bw = bsz * w
    rows = steps * bsz
    const4 = lambda i: (0, 0, 0, 0)
    const3 = lambda i: (0, 0, 0)
    const2 = lambda i: (0, 0)
    return pl.pallas_call(
        _s5_kernel,
        grid=(seq // steps,),
        in_specs=[pl.BlockSpec((steps, bw), lambda i: (i, 0)),
                  pl.BlockSpec((steps, bw), lambda i: (i, 1)),
                  pl.BlockSpec(b4.shape, const4),
                  pl.BlockSpec(lam2.shape, const3),
                  pl.BlockSpec(c2.shape, const4),
                  pl.BlockSpec((1, w), const2),
                  pl.BlockSpec((w, w), const2),
                  pl.BlockSpec((1, w), const2)],
        out_specs=pl.BlockSpec((steps, bw), lambda i: (i, 0)),
        out_shape=jax.ShapeDtypeStruct((seq, bw), bf16),
        scratch_shapes=[pltpu.VMEM((S5_NGB, 2 * LANES, S5_SW), bf16),
                        pltpu.VMEM((S5_NGB, S5_SW, LANES), bf16),
                        pltpu.VMEM((S5_NGB, rows + SUBLANES, LANES), f32),
                        pltpu.VMEM((2, rows, S5_SW), f32),
                        pltpu.VMEM((2, rows, S5_SW), f32),
                        pltpu.VMEM((S5_NGB, rows, LANES), f32),
                        pltpu.VMEM((S5_NGB, SUBLANES, S5_SW), f32)],
        compiler_params=_cparams("arbitrary"),
        name="s5_mixer",
    )(uz, uz, b4, lam2, c2, d.reshape(1, w), glu_w, glu_b.reshape(1, w))


def _cast_body(n_arrays, step, *refs):
    del step

    def cast():
        for src, dst in zip(refs[:n_arrays], refs[n_arrays:]):
            dst[...] = src[...].astype(dst.dtype)

    return [cast]


def _cast_job(arrays, layer, steps, n_blocks=CAST_BLOCKS):
    block = lambda step: (jnp.minimum(step, n_blocks - 1), 0)
    specs, shapes = [], []
    for m in arrays:
        rows = m.shape[1] // n_blocks
        assert rows * n_blocks == m.shape[1] and rows % (2 * SUBLANES) == 0
        specs.append(((pl.Squeezed(), rows, m.shape[2]),
                      lambda step: (layer, jnp.minimum(step, n_blocks - 1), 0)))
        shapes.append(((rows, m.shape[2]), jax.ShapeDtypeStruct(m.shape[1:], bf16)))
    return _SideJob(
        steps=steps,
        body=functools.partial(_cast_body, len(arrays)),
        inputs=tuple(arrays),
        in_specs=tuple(specs),
        out_shapes=tuple(sh for _, sh in shapes),
        out_specs=tuple((blk, block) for blk, _ in shapes),
        scratch=())


def _gmlp_body(step, u_ref, v_ref, z_ref, lnw_ref, lnb_ref, ws_ref, bias_ref, o_ref,
               vn_ref, wm_ref):
    del step
    rows = u_ref.shape[0]

    def norm_part(c):
        rs = slice(c * SG_CHUNK, (c + 1) * SG_CHUNK)
        if c == 0:
            t_idx = lax.broadcasted_iota(jnp.int32, (SG_CHUNK, SG_CHUNK), 0)
            s_idx = lax.broadcasted_iota(jnp.int32, (SG_CHUNK, SG_CHUNK), 1)
            for h in range(SG_HEADS):
                wm_ref[h] = jnp.where(s_idx <= t_idx, ws_ref[h], 0.0).astype(bf16)
        v = v_ref[rs, :].astype(f32)
        mu = jnp.mean(v, axis=-1, keepdims=True)
        vc = v - mu
        var = jnp.mean(vc * vc, axis=-1, keepdims=True)
        vn_ref[rs, :] = (vc * lax.rsqrt(var + EPS) * lnw_ref[...] + lnb_ref[...]).astype(bf16)

    def mix_part(c, h):
        rs = slice(c * SG_CHUNK, (c + 1) * SG_CHUNK)
        cs = slice(h * LANES, (h + 1) * LANES)
        mixed = jnp.dot(wm_ref[h], vn_ref[rs, cs], preferred_element_type=f32)
        mixed = mixed + bias_ref[:, cs]
        o_ref[rs, cs] = (u_ref[rs, cs].astype(f32) * mixed
                         * z_ref[rs, cs].astype(f32)).astype(o_ref.dtype)

    parts = []
    for c in range(rows // SG_CHUNK):
        parts.append(functools.partial(norm_part, c))
        parts.extend(functools.partial(mix_part, c, h) for h in range(SG_HEADS))
    return parts


def _gmlp_job(p, ln_w, ln_b, w_s, b_s, rows=GMLP_ROWS):
    t = p.shape[0]
    w = SG_WIDTH
    bias = jnp.repeat(b_s.T, LANES, axis=1)
    const2 = lambda step: (0, 0)
    return _SideJob(
        steps=t // rows,
        body=_gmlp_body,
        inputs=(p, p, p, ln_w.reshape(1, w), ln_b.reshape(1, w), w_s, bias),
        in_specs=(((rows, w), lambda step: (step, 0)),
                  ((rows, w), lambda step: (step, 1)),
                  ((rows, w), lambda step: (step, 2)),
                  ((1, w), const2),
                  ((1, w), const2),
                  (w_s.shape, lambda step: (0, 0, 0)),
                  ((SG_CHUNK, w), const2)),
        out_shapes=(jax.ShapeDtypeStruct((t, w), bf16),),
        out_specs=(((rows, w), lambda step: (step, 0)),),
        scratch=(pltpu.VMEM((rows, w), bf16),
                 pltpu.VMEM((SG_HEADS, SG_CHUNK, SG_CHUNK), bf16)))


def _rope(x, cos, sin_up, sin_dn):
    half = ROT_DIM // 2
    return (x * cos + pltpu.roll(x, LANES - half, 1) * sin_up
            + pltpu.roll(x, half, 1) * sin_dn)


def _attn_body(nb, step, sink_ref, q_ref, z_ref, kvp_ref, kvc_ref, rp_ref, rc_ref, o_ref):
    n = step % nb
    blk = ATT_BLOCK
    group = ATT_HEADS // ATT_KV_HEADS
    pre = {}

    def prepare():
        cos_c, up_c, dn_c = rc_ref[0], rc_ref[1], rc_ref[2]
        cos_p, up_p, dn_p = rp_ref[0], rp_ref[1], rp_ref[2]
        k_all = jnp.concatenate([
            _rope(kvp_ref[:, :LANES].astype(f32), cos_p, up_p, dn_p),
            _rope(kvc_ref[:, :LANES].astype(f32), cos_c, up_c, dn_c)], axis=0)
        v_all = jnp.concatenate([kvp_ref[:, LANES:].astype(f32),
                                 kvc_ref[:, LANES:].astype(f32)], axis=0)
        lane = lax.broadcasted_iota(jnp.int32, (2 * blk, LANES), 1)
        low = lane < HEAD_DIM
        k_sw = pltpu.roll(k_all, HEAD_DIM, 1)
        v_sw = pltpu.roll(v_all, HEAD_DIM, 1)
        pre["k"] = [[jnp.where(low, k_all, 0.0).astype(bf16),
                     jnp.where(low, 0.0, k_sw).astype(bf16)],
                    [jnp.where(low, k_sw, 0.0).astype(bf16),
                     jnp.where(low, 0.0, k_all).astype(bf16)]]
        pre["v"] = [[jnp.where(low, v_all, 0.0).astype(bf16),
                     jnp.where(low, 0.0, v_sw).astype(bf16)],
                    [jnp.where(low, v_sw, 0.0).astype(bf16),
                     jnp.where(low, 0.0, v_all).astype(bf16)]]
        qi = lax.broadcasted_iota(jnp.int32, (blk, blk), 0)
        kj = lax.broadcasted_iota(jnp.int32, (blk, blk), 1)
        pre["in_cur"] = kj <= qi
        pre["prev_bias"] = jnp.where(pre["in_cur"] | (n > 0), 0.0, NEG_INF)
        scale = HEAD_DIM ** -0.5
        pre["rope_q"] = (cos_c * scale, up_c * scale, dn_c * scale)

    def staged(tile_lo, tile_hi):
        heads = range(2 * tile_lo, 2 * tile_hi)
        tiles = {j: slice(j * LANES, (j + 1) * LANES) for j in range(tile_lo, tile_hi)}
        st = {}

        def scores():
            qs = {j: _rope(q_ref[:, cs].astype(f32), *pre["rope_q"]).astype(bf16)
                  for j, cs in tiles.items()}
            s2 = {hd: lax.dot_general(qs[hd // 2], pre["k"][hd // group][hd % 2],
                                      (((1,), (1,)), ((), ())), preferred_element_type=f32)
                  for hd in heads}
            st["s"] = {hd: jnp.where(pre["in_cur"], s2[hd][:, blk:], s2[hd][:, :blk])
                       + pre["prev_bias"] for hd in heads}

        def row_max():
            st["m"] = {hd: jnp.maximum(jnp.max(st["s"][hd], axis=-1, keepdims=True), sink_ref[hd])
                       for hd in heads}

        def exponent():
            st["p"] = {hd: jnp.exp(st["s"][hd] - st["m"][hd]) for hd in heads}
            st["den"] = {hd: jnp.sum(st["p"][hd], axis=-1, keepdims=True)
                         + jnp.exp(sink_ref[hd] - st["m"][hd]) for hd in heads}

        def values():
            p, in_cur = st["p"], pre["in_cur"]
            p2 = {hd: jnp.concatenate([jnp.where(in_cur, 0.0, p[hd]),
                                       jnp.where(in_cur, p[hd], 0.0)], axis=1).astype(bf16)
                  for hd in heads}
            st["pv"] = {hd: jnp.dot(p2[hd], pre["v"][hd // group][hd % 2],
                                    preferred_element_type=f32) * (1.0 / st["den"][hd])
                        for hd in heads}

        def output():
            pv = st["pv"]
            for j, cs in tiles.items():
                o_ref[:, cs] = ((pv[2 * j] + pv[2 * j + 1])
                                * z_ref[:, cs].astype(f32)).astype(o_ref.dtype)

        return [scores, row_max, exponent, values, output]

    n_tiles = ATT_HEADS // 2
    bounds = [0, 3, 6, n_tiles]
    return [prepare] + [stage for lo, hi in zip(bounds[:-1], bounds[1:])
                        for stage in staged(lo, hi)]


def _rope_tables(seq):
    half = ROT_DIM // 2
    inv_freq = ROPE_THETA ** (-jnp.arange(0, ROT_DIM, 2, dtype=f32) / ROT_DIM)
    ang = jnp.arange(seq, dtype=f32)[:, None] * inv_freq[None, :]
    cos, sin = jnp.cos(ang), jnp.sin(ang)
    ones = jnp.ones((seq, HEAD_DIM - ROT_DIM), f32)
    zeros = jnp.zeros((seq, HEAD_DIM - ROT_DIM), f32)
    zh = jnp.zeros((seq, half), f32)
    cos_t = jnp.concatenate([cos, cos, ones], axis=1)
    up_t = jnp.concatenate([-sin, zh, zeros], axis=1)
    dn_t = jnp.concatenate([zh, sin, zeros], axis=1)
    tab = jnp.stack([cos_t, up_t, dn_t])
    return jnp.concatenate([tab, tab], axis=2)


def _attn_job(p, kv, sinks, rope, bsz, seq):
    nb = seq // ATT_BLOCK
    w = ATT_WIDTH
    prev = lambda step: step - jnp.minimum(step % nb, 1)
    return _SideJob(
        steps=bsz * nb,
        body=functools.partial(_attn_body, nb),
        inputs=(sinks, p, p, kv, kv, rope, rope),
        in_specs=(None,
                  ((ATT_BLOCK, w), lambda step: (step, 3)),
                  ((ATT_BLOCK, w), lambda step: (step, 4)),
                  ((ATT_BLOCK, 2 * KV_WIDTH), lambda step: (prev(step), 0)),
                  ((ATT_BLOCK, 2 * KV_WIDTH), lambda step: (step, 0)),
                  ((3, ATT_BLOCK, LANES), lambda step: (0, prev(step) % nb, 0)),
                  ((3, ATT_BLOCK, LANES), lambda step: (0, step % nb, 0))),
        out_shapes=(jax.ShapeDtypeStruct((bsz * seq, w), bf16),),
        out_specs=(((ATT_BLOCK, w), lambda step: (step, 0)),),
        scratch=())


def _merge_kernel(ya_ref, yb_ref, yc_ref, g_ref, x_ref, wa_ref, wb_ref, wc_ref, wo_ref,
                  nw_ref, *out_refs, last, chunk):
    merged_ref = out_refs[-1]
    ys = (ya_ref, yb_ref, yc_ref)
    ws = (wa_ref, wb_ref, wc_ref)
    for c in range(D_MODEL // chunk):
        cs = slice(c * chunk, (c + 1) * chunk)
        acc = None
        for k in range(N_BRANCH):
            gate = jax.nn.sigmoid(g_ref[:, k * D_MODEL + c * chunk:k * D_MODEL + (c + 1) * chunk]
                                  .astype(f32))
            term = gate * jnp.dot(ys[k][...], ws[k][:, cs], preferred_element_type=f32)
            acc = term if acc is None else acc + term
        merged_ref[:, cs] = acc.astype(bf16)
    x_new = x_ref[...] + jnp.dot(merged_ref[...], wo_ref[...], preferred_element_type=f32)
    ms = jnp.mean(x_new * x_new, axis=-1, keepdims=True)
    normed = x_new * lax.rsqrt(ms + EPS) * nw_ref[...]
    if last:
        out_refs[0][...] = normed
    else:
        out_refs[0][...] = x_new
        out_refs[1][...] = normed.astype(bf16)


def _merge_out(ya_tm, yb, yc, gates, x, wa, wb, wc, wo, next_norm_w, bsz, seq, last,
               tm=MERGE_ROWS):
    t = bsz * seq
    nt = seq // tm
    d = D_MODEL
    row = lambda b, n: (b * nt + n, 0)
    const = lambda b, n: (0, 0)
    once = pl.Buffered(1)
    wspec = lambda m: pl.BlockSpec(m.shape, const, pipeline_mode=once)
    if last:
        out_shape = [jax.ShapeDtypeStruct((t, d), f32)]
        out_specs = [pl.BlockSpec((tm, d), row)]
    else:
        out_shape = [jax.ShapeDtypeStruct((t, d), f32), jax.ShapeDtypeStruct((t, d), bf16)]
        out_specs = [pl.BlockSpec((tm, d), row), pl.BlockSpec((tm, d), row)]
    return pl.pallas_call(
        functools.partial(_merge_kernel, last=last, chunk=MERGE_CHUNK),
        grid=(bsz, nt),
        in_specs=[pl.BlockSpec((tm, SSM_WIDTH), lambda b, n: (n, b)),
                  pl.BlockSpec((tm, SG_WIDTH), row),
                  pl.BlockSpec((tm, ATT_WIDTH), row),
                  pl.BlockSpec((tm, N_BRANCH * d), row),
                  pl.BlockSpec((tm, d), row),
                  wspec(wa), wspec(wb), wspec(wc), wspec(wo),
                  pl.BlockSpec((1, d), const)],
        out_specs=out_specs,
        out_shape=out_shape,
        scratch_shapes=[pltpu.VMEM((tm, d), bf16)],
        compiler_params=_cparams("parallel", "parallel"),
        name="merge_out",
    )(ya_tm, yb, yc, gates, x, wa, wb, wc, wo, next_norm_w.reshape(1, d))


def kernel(x, norm_w, w_in, ssm_a_re, ssm_a_im, ssm_log_dt, ssm_b_re, ssm_b_im, ssm_c_re, ssm_c_im, ssm_d, ssm_glu_w, ssm_glu_b, sg_ln_w, sg_ln_b, sg_w, sg_b, attn_sinks, w_branch_a, w_branch_b, w_branch_c, w_out, final_norm_w):
    bsz, seq, d = x.shape
    assert 2 * bsz == SUBLANES, "the S5 kernel packs two timesteps of all batches per sublane tile"
    depth = norm_w.shape[0]
    t = bsz * seq
    xf = x.reshape(t, d)
    rope = _rope_tables(seq)

    o_ub = 2 * SSM_WIDTH
    o_q = o_ub + 3 * SG_WIDTH
    o_k = o_q + ATT_WIDTH
    o_zc = o_k + 2 * KV_WIDTH
    o_g = o_zc + ATT_WIDTH

    tn = PROJ_COLS
    gate_tn = GATE_COLS
    s5_cols = [0, SSM_WIDTH]
    bc_cols = [o_ub, o_ub + tn, o_ub + 2 * tn, o_q, o_zc]
    gate_cols = [o_g + c * gate_tn for c in range(N_BRANCH * d // gate_tn)]

    b4s, lam2s, c2s = jax.vmap(_s5_params)(ssm_a_re, ssm_a_im, ssm_log_dt, ssm_b_re,
                                           ssm_b_im, ssm_c_re, ssm_c_im)
    layer_weights = (ssm_glu_w, w_branch_a, w_branch_b, w_branch_c, w_out)

    h = _rmsnorm(xf, norm_w[0], bf16)
    out = None
    for l in range(depth):
        p_bc, gw, wa, wb, wc, wo = _project(
            h, w_in, l, bc_cols, tn,
            side=_cast_job(layer_weights, l, steps=len(bc_cols) * (t // PROJ_ROWS)),
            activations=["gelu", "gelu", "silu", None, "silu"], name="proj_bc")
        kv = _project(h, w_in, l, [o_k], 2 * KV_WIDTH, name="proj_kv")
        uz, yb = _project(h, w_in, l, s5_cols, tn, time_major_batches=bsz,
                          side=_gmlp_job(p_bc, sg_ln_w[l], sg_ln_b[l], sg_w[l], sg_b[l]),
                          activations=[None, "silu"], name="proj_s5_gmlp")
        gates, yc = _project(h, w_in, l, gate_cols, gate_tn,
                             side=_attn_job(p_bc, kv, attn_sinks[l], rope, bsz, seq),
                             name="proj_gates_attn")
        ya = _s5_mixer(uz, b4s[l], lam2s[l], c2s[l], ssm_d[l], gw, ssm_glu_b[l], bsz)

        last = l == depth - 1
        next_w = final_norm_w if last else norm_w[l + 1]
        res = _merge_out(ya, yb, yc, gates, xf, wa, wb, wc, wo, next_w, bsz, seq, last)
        if last:
            out = res[0]
        else:
            xf, h = res
    return out.reshape(bsz, seq, d)
```

```python
import functools
from typing import Callable, NamedTuple

import jax
import jax.numpy as jnp
from jax import lax
from jax.experimental import pallas as pl
from jax.experimental.pallas import tpu as pltpu

f32 = jnp.float32
bf16 = jnp.bfloat16

D_MODEL = 2048
EPS = 1e-6
NEG_INF = -1e30

SSM_WIDTH = D_MODEL // 2
SSM_GROUP = 16
SSM_GROUPS = SSM_WIDTH // SSM_GROUP
SSM_STATE = 64

SG_WIDTH = D_MODEL // 2
SG_HEADS = 8
SG_CHUNK = 128

HEAD_DIM = 64
ATT_HEADS = D_MODEL // 128
ATT_KV_HEADS = ATT_HEADS // 8
ATT_WIDTH = ATT_HEADS * HEAD_DIM
KV_WIDTH = ATT_KV_HEADS * HEAD_DIM
ATT_BLOCK = 128
ROT_DIM = HEAD_DIM // 4
ROPE_THETA = 500000.0
N_BRANCH = 3

LANES = 128
SUBLANES = 8
MXU_DIM = 256
VMEM_LIMIT_BYTES = 48 * 1024 * 1024

NORM_ROWS = 512
PROJ_ROWS = 1024
PROJ_COLS = 1024
GATE_COLS = 768
GMLP_ROWS = 512
S5_STEPS = 128
CAST_BLOCKS = 32
MERGE_ROWS = 256
MERGE_CHUNK = 512

S5_GB = LANES // SSM_GROUP
S5_NGB = SSM_GROUPS // S5_GB
S5_HALF = S5_GB * SSM_STATE
S5_SW = 2 * S5_HALF


def _cparams(*sem):
    return pltpu.CompilerParams(dimension_semantics=sem, vmem_limit_bytes=VMEM_LIMIT_BYTES)


def _silu(x):
    return x * jax.nn.sigmoid(x)


def _rmsnorm_kernel(x_ref, w_ref, o_ref):
    x = x_ref[...]
    ms = jnp.mean(x * x, axis=-1, keepdims=True)
    o_ref[...] = (x * lax.rsqrt(ms + EPS) * w_ref[...]).astype(o_ref.dtype)


def _rmsnorm(x, w, out_dtype, tm=NORM_ROWS):
    t, d = x.shape
    return pl.pallas_call(
        _rmsnorm_kernel,
        grid=(t // tm,),
        in_specs=[pl.BlockSpec((tm, d), lambda i: (i, 0)),
                  pl.BlockSpec((1, d), lambda i: (0, 0))],
        out_specs=pl.BlockSpec((tm, d), lambda i: (i, 0)),
        out_shape=jax.ShapeDtypeStruct((t, d), out_dtype),
        compiler_params=_cparams("parallel"),
        name="rmsnorm",
    )(x, w.reshape(1, d))


class _SideJob(NamedTuple):
    steps: int
    body: Callable
    inputs: tuple
    in_specs: tuple
    out_shapes: tuple
    out_specs: tuple
    scratch: tuple


_ACTIVATIONS = {None: lambda r: r, "gelu": jax.nn.gelu, "silu": _silu}


def _proj_kernel(offs_ref, h_ref, w_ref, *refs, side_body, n_side_in, n_side_out, activations):
    del offs_ref
    side_in = refs[:n_side_in]
    o_ref = refs[n_side_in]
    side_out = refs[n_side_in + 1:n_side_in + 1 + n_side_out]
    wbf_ref = refs[n_side_in + 1 + n_side_out]
    stage_ref = refs[n_side_in + 2 + n_side_out]
    side_scratch = refs[n_side_in + 3 + n_side_out:]

    @pl.when(pl.program_id(1) == 0)
    def _():
        wbf_ref[...] = w_ref[...].astype(bf16)

    def run(name):
        act = _ACTIVATIONS[name]
        tm, tn = o_ref.shape
        if side_body is None:
            o_ref[...] = act(jnp.dot(h_ref[...], wbf_ref[...],
                                     preferred_element_type=f32)).astype(o_ref.dtype)
            return
        step = pl.program_id(0) * pl.num_programs(1) + pl.program_id(1)
        parts = side_body(step, *side_in, *side_out, *side_scratch)
        pieces = [(rs, cs) for cs in range(0, tn, MXU_DIM) for rs in range(0, tm, MXU_DIM)]
        n_parts, n_gaps = len(parts), len(pieces) - 1
        done = 0
        behind = None
        for k, (rs, cs) in enumerate(pieces):
            acc = jnp.dot(h_ref[rs:rs + MXU_DIM, :], wbf_ref[:, cs:cs + MXU_DIM],
                          preferred_element_type=f32)
            if name is None:
                o_ref[rs:rs + MXU_DIM, cs:cs + MXU_DIM] = acc.astype(o_ref.dtype)
            else:
                stage_ref[k % 2] = acc
                if behind is not None:
                    prs, pcs = behind
                    o_ref[prs:prs + MXU_DIM, pcs:pcs + MXU_DIM] = act(
                        stage_ref[(k - 1) % 2]).astype(o_ref.dtype)
                behind = (rs, cs)
            upto = min(n_parts, -(-(k + 1) * n_parts // n_gaps))
            for part in parts[done:upto]:
                part()
            done = upto
        if behind is not None:
            prs, pcs = behind
            o_ref[prs:prs + MXU_DIM, pcs:pcs + MXU_DIM] = act(
                stage_ref[(len(pieces) - 1) % 2]).astype(o_ref.dtype)

    blocks_of = {}
    for jb, name in enumerate(activations):
        blocks_of.setdefault(name, []).append(jb)
    if len(blocks_of) == 1:
        run(activations[0])
    else:
        j = pl.program_id(0)
        for name, blocks in blocks_of.items():
            pl.when(functools.reduce(jnp.logical_or, [j == jb for jb in blocks]))(
                functools.partial(run, name))


def _project(h, w_all, layer, col_offsets, tn, tm=PROJ_ROWS, time_major_batches=None, side=None,
             activations=None, name="proj"):
    t, k = h.shape
    nj = len(col_offsets)
    ni = t // tm
    assert all(off % LANES == 0 for off in col_offsets)
    assert side is None or side.steps == nj * ni
    offs = jnp.asarray([off // LANES for off in col_offsets], jnp.int32)
    if time_major_batches is None:
        out_shape = (t, nj * tn)
        out_map = lambda j, i, offs_ref: (i, j)
    else:
        seq = t // time_major_batches
        nt = seq // tm
        out_shape = (seq, nj * time_major_batches * tn)
        out_map = lambda j, i, offs_ref: (i % nt, j * time_major_batches + i // nt)

    def step_spec(spec):
        if spec is None:
            return pl.BlockSpec(memory_space=pltpu.SMEM)
        shape, index = spec
        return pl.BlockSpec(shape, lambda j, i, offs_ref: index(j * ni + i))

    side_in_specs = [] if side is None else [step_spec(sp) for sp in side.in_specs]
    side_out_specs = [] if side is None else [step_spec(sp) for sp in side.out_specs]
    side_inputs = () if side is None else side.inputs
    side_out_shapes = [] if side is None else list(side.out_shapes)
    side_scratch = [] if side is None else list(side.scratch)
    grid_spec = pltpu.PrefetchScalarGridSpec(
        num_scalar_prefetch=1,
        grid=(nj, ni),
        in_specs=[pl.BlockSpec((tm, k), lambda j, i, offs_ref: (i, 0)),
                  pl.BlockSpec((pl.Squeezed(), pl.Element(k), pl.Element(tn)),
                               lambda j, i, offs_ref: (layer, 0, offs_ref[j] * LANES))]
                 + side_in_specs,
        out_specs=[pl.BlockSpec((tm, tn), out_map)] + side_out_specs,
        scratch_shapes=[pltpu.VMEM((k, tn), bf16),
                        pltpu.VMEM((2, MXU_DIM, MXU_DIM), f32)] + side_scratch)
    res = pl.pallas_call(
        functools.partial(_proj_kernel, side_body=None if side is None else side.body,
                          n_side_in=len(side_in_specs), n_side_out=len(side_out_specs),
                          activations=tuple(activations or [None] * nj)),
        grid_spec=grid_spec,
        out_shape=[jax.ShapeDtypeStruct(out_shape, bf16)] + side_out_shapes,
        compiler_params=_cparams("arbitrary", "arbitrary"),
        name=name,
    )(offs, h, w_all, *side_inputs)
    return res[0] if side is None else res


def _tile_lanes(x, width):
    while width < LANES:
        x = x + pltpu.roll(x, width, 1)
        width *= 2
    return x


def _s5_kernel(u_ref, z_ref, b4_ref, lam2_ref, c2_ref, d_ref, gw_ref, gb_ref, o_ref,
               w2_ref, cm_ref, ubuf, bu2_ref, st2_ref, y_ref, carry_ref):
    i = pl.program_id(0)
    steps = u_ref.shape[0]
    bsz = SUBLANES // 2
    rows = steps * bsz
    w = SSM_WIDTH

    @pl.when(i == 0)
    def _():
        carry_ref[...] = jnp.zeros_like(carry_ref)
        ubuf[:, 0:SUBLANES, :] = jnp.zeros((S5_NGB, SUBLANES, LANES), f32)
        in_diag = (lax.broadcasted_iota(jnp.int32, (LANES, S5_HALF), 0) // SSM_GROUP
                   == lax.broadcasted_iota(jnp.int32, (LANES, S5_HALF), 1) // SSM_STATE)
        out_diag = (lax.broadcasted_iota(jnp.int32, (S5_HALF, LANES), 0) // SSM_STATE
                    == lax.broadcasted_iota(jnp.int32, (S5_HALF, LANES), 1) // SSM_GROUP)
        for g in range(S5_NGB):
            for k in range(4):
                x = _tile_lanes(b4_ref[k, g], SSM_STATE)
                x = jnp.concatenate([x] * (S5_HALF // LANES), axis=1)
                w2_ref[g, (k // 2) * LANES:(k // 2 + 1) * LANES,
                       (k % 2) * S5_HALF:(k % 2 + 1) * S5_HALF] = (
                           jnp.where(in_diag, x, 0.0).astype(bf16))
            for k in range(2):
                x = _tile_lanes(c2_ref[k, g], SSM_GROUP)
                cm_ref[g, k * S5_HALF:(k + 1) * S5_HALF, :] = (
                    jnp.where(out_diag, x, 0.0).astype(bf16))

    @pl.when(i > 0)
    def _():
        ubuf[:, 0:SUBLANES, :] = ubuf[:, rows:rows + SUBLANES, :]

    for g in range(S5_NGB):
        cs = slice(g * LANES, (g + 1) * LANES)
        for b in range(bsz):
            ubuf[g, pl.ds(SUBLANES + b, steps, stride=bsz), :] = (
                u_ref[:, b * w + g * LANES:b * w + (g + 1) * LANES].astype(f32))
        lhs = jnp.concatenate(
            [ubuf[g, SUBLANES:rows + SUBLANES, :].astype(bf16),
             ubuf[g, SUBLANES - bsz:rows + SUBLANES - bsz, :].astype(bf16)], axis=1)
        bu_ref = bu2_ref.at[g % 2]
        st_ref = st2_ref.at[g % 2]
        bu_ref[...] = jnp.dot(lhs, w2_ref[g], preferred_element_type=f32)
        l_re = lam2_ref[g, :, :S5_HALF]
        l_im = lam2_ref[g, :, S5_HALF:]

        def body(r, s, l_re=l_re, l_im=l_im):
            s_re, s_im = s
            off = pl.multiple_of(r * SUBLANES, SUBLANES)
            b = bu_ref[pl.ds(off, SUBLANES), :]
            n_re = l_re * s_re - l_im * s_im + b[:, :S5_HALF]
            n_im = l_re * s_im + l_im * s_re + b[:, S5_HALF:]
            st_ref[pl.ds(off, SUBLANES), :S5_HALF] = n_re
            st_ref[pl.ds(off, SUBLANES), S5_HALF:] = n_im
            return n_re, n_im

        s0 = (carry_ref[g, :, :S5_HALF], carry_ref[g, :, S5_HALF:])
        s_re, s_im = lax.fori_loop(0, rows // SUBLANES, body, s0, unroll=True)
        carry_ref[g, :, :S5_HALF] = s_re
        carry_ref[g, :, S5_HALF:] = s_im
        y_g = jnp.dot(st_ref[...].astype(bf16), cm_ref[g], preferred_element_type=f32)
        y_ref[g] = y_g + d_ref[:, cs] * ubuf[g, SUBLANES:rows + SUBLANES, :]

    y = jax.nn.gelu(jnp.concatenate([y_ref[g] for g in range(S5_NGB)], axis=1))
    gate = jnp.dot(y.astype(bf16), gw_ref[...], preferred_element_type=f32) + gb_ref[...]
    y = y * jax.nn.sigmoid(gate)
    for g in range(S5_NGB):
        y_ref[g] = y[:, g * LANES:(g + 1) * LANES]
    for b in range(bsz):
        for g in range(S5_NGB):
            cs = slice(b * w + g * LANES, b * w + (g + 1) * LANES)
            o_ref[:, cs] = (y_ref[g, pl.ds(b, steps, stride=bsz), :]
                            * z_ref[:, cs].astype(f32)).astype(o_ref.dtype)


def _s5_params(a_re, a_im, log_dt, b_re, b_im, c_re, c_im):
    dt = jnp.exp(log_dt)[:, None]
    mag = jnp.exp(a_re * dt)
    lb_re = mag * jnp.cos(a_im * dt)
    lb_im = mag * jnp.sin(a_im * dt)
    den = a_re * a_re + a_im * a_im
    k_re = ((lb_re - 1.0) * a_re + lb_im * a_im) / den
    k_im = (lb_im * a_re - (lb_re - 1.0) * a_im) / den
    bb_re = k_re[..., None] * b_re - k_im[..., None] * b_im
    bb_im = k_re[..., None] * b_im + k_im[..., None] * b_re
    t_re = lb_re[..., None] * bb_re - lb_im[..., None] * bb_im
    t_im = lb_re[..., None] * bb_im + lb_im[..., None] * bb_re

    def pad_lanes(m):
        return jnp.pad(m, ((0, 0), (0, 0), (0, LANES - m.shape[-1])))

    b4 = jnp.stack([pad_lanes(m.transpose(0, 2, 1).reshape(S5_NGB, LANES, SSM_STATE))
                    for m in (bb_re, bb_im, t_re, t_im)])
    c2 = jnp.stack([pad_lanes(m.transpose(0, 2, 1).reshape(S5_NGB, S5_HALF, SSM_GROUP))
                    for m in (c_re, -c_im)])
    l2_re = (lb_re * lb_re - lb_im * lb_im).reshape(S5_NGB, 1, S5_HALF)
    l2_im = (2.0 * lb_re * lb_im).reshape(S5_NGB, 1, S5_HALF)
    lam2 = jnp.broadcast_to(jnp.concatenate([l2_re, l2_im], axis=2),
                            (S5_NGB, SUBLANES, S5_SW))
    return b4, lam2, c2


def _s5_mixer(uz, b4, lam2, c2, d, glu_w, glu_b, bsz, steps=S5_STEPS):
    seq = uz.shape[0]
    w = SSM_WIDTH
    bw = bsz * w
    rows = steps * bsz
    const4 = lambda i: (0, 0, 0, 0)
    const3 = lambda i: (0, 0, 0)
    const2 = lambda i: (0, 0)
    return pl.pallas_call(
        _s5_kernel,
        grid=(seq // steps,),
        in_specs=[pl.BlockSpec((steps, bw), lambda i: (i, 0)),
                  pl.BlockSpec((steps, bw), lambda i: (i, 1)),
                  pl.BlockSpec(b4.shape, const4),
                  pl.BlockSpec(lam2.shape, const3),
                  pl.BlockSpec(c2.shape, const4),
                  pl.BlockSpec((1, w), const2),
                  pl.BlockSpec((w, w), const2),
                  pl.BlockSpec((1, w), const2)],
        out_specs=pl.BlockSpec((steps, bw), lambda i: (i, 0)),
        out_shape=jax.ShapeDtypeStruct((seq, bw), bf16),
        scratch_shapes=[pltpu.VMEM((S5_NGB, 2 * LANES, S5_SW), bf16),
                        pltpu.VMEM((S5_NGB, S5_SW, LANES), bf16),
                        pltpu.VMEM((S5_NGB, rows + SUBLANES, LANES), f32),
                        pltpu.VMEM((2, rows, S5_SW), f32),
                        pltpu.VMEM((2, rows, S5_SW), f32),
                        pltpu.VMEM((S5_NGB, rows, LANES), f32),
                        pltpu.VMEM((S5_NGB, SUBLANES, S5_SW), f32)],
        compiler_params=_cparams("arbitrary"),
        name="s5_mixer",
    )(uz, uz, b4, lam2, c2, d.reshape(1, w), glu_w, glu_b.reshape(1, w))


def _cast_body(n_arrays, step, *refs):
    del step

    def cast():
        for src, dst in zip(refs[:n_arrays], refs[n_arrays:]):
            dst[...] = src[...].astype(dst.dtype)

    return [cast]


def _cast_job(arrays, layer, steps, n_blocks=CAST_BLOCKS):
    block = lambda step: (jnp.minimum(step, n_blocks - 1), 0)
    specs, shapes = [], []
    for m in arrays:
        rows = m.shape[1] // n_blocks
        assert rows * n_blocks == m.shape[1] and rows % (2 * SUBLANES) == 0
        specs.append(((pl.Squeezed(), rows, m.shape[2]),
                      lambda step: (layer, jnp.minimum(step, n_blocks - 1), 0)))
        shapes.append(((rows, m.shape[2]), jax.ShapeDtypeStruct(m.shape[1:], bf16)))
    return _SideJob(
        steps=steps,
        body=functools.partial(_cast_body, len(arrays)),
        inputs=tuple(arrays),
        in_specs=tuple(specs),
        out_shapes=tuple(sh for _, sh in shapes),
        out_specs=tuple((blk, block) for blk, _ in shapes),
        scratch=())


def _gmlp_body(step, u_ref, v_ref, z_ref, lnw_ref, lnb_ref, ws_ref, bias_ref, o_ref,
               vn_ref, wm_ref):
    del step
    rows = u_ref.shape[0]

    def norm_part(c):
        rs = slice(c * SG_CHUNK, (c + 1) * SG_CHUNK)
        if c == 0:
            t_idx = lax.broadcasted_iota(jnp.int32, (SG_CHUNK, SG_CHUNK), 0)
            s_idx = lax.broadcasted_iota(jnp.int32, (SG_CHUNK, SG_CHUNK), 1)
            for h in range(SG_HEADS):
                wm_ref[h] = jnp.where(s_idx <= t_idx, ws_ref[h], 0.0).astype(bf16)
        v = v_ref[rs, :].astype(f32)
        mu = jnp.mean(v, axis=-1, keepdims=True)
        vc = v - mu
        var = jnp.mean(vc * vc, axis=-1, keepdims=True)
        vn_ref[rs, :] = (vc * lax.rsqrt(var + EPS) * lnw_ref[...] + lnb_ref[...]).astype(bf16)

    def mix_part(c, h):
        rs = slice(c * SG_CHUNK, (c + 1) * SG_CHUNK)
        cs = slice(h * LANES, (h + 1) * LANES)
        mixed = jnp.dot(wm_ref[h], vn_ref[rs, cs], preferred_element_type=f32)
        mixed = mixed + bias_ref[:, cs]
        o_ref[rs, cs] = (u_ref[rs, cs].astype(f32) * mixed
                         * z_ref[rs, cs].astype(f32)).astype(o_ref.dtype)

    parts = []
    for c in range(rows // SG_CHUNK):
        parts.append(functools.partial(norm_part, c))
        parts.extend(functools.partial(mix_part, c, h) for h in range(SG_HEADS))
    return parts


def _gmlp_job(p, ln_w, ln_b, w_s, b_s, rows=GMLP_ROWS):
    t = p.shape[0]
    w = SG_WIDTH
    bias = jnp.repeat(b_s.T, LANES, axis=1)
    const2 = lambda step: (0, 0)
    return _SideJob(
        steps=t // rows,
        body=_gmlp_body,
        inputs=(p, p, p, ln_w.reshape(1, w), ln_b.reshape(1, w), w_s, bias),
        in_specs=(((rows, w), lambda step: (step, 0)),
                  ((rows, w), lambda step: (step, 1)),
                  ((rows, w), lambda step: (step, 2)),
                  ((1, w), const2),
                  ((1, w), const2),
                  (w_s.shape, lambda step: (0, 0, 0)),
                  ((SG_CHUNK, w), const2)),
        out_shapes=(jax.ShapeDtypeStruct((t, w), bf16),),
        out_specs=(((rows, w), lambda step: (step, 0)),),
        scratch=(pltpu.VMEM((rows, w), bf16),
                 pltpu.VMEM((SG_HEADS, SG_CHUNK, SG_CHUNK), bf16)))


def _rope(x, cos, sin_up, sin_dn):
    half = ROT_DIM // 2
    return (x * cos + pltpu.roll(x, LANES - half, 1) * sin_up
            + pltpu.roll(x, half, 1) * sin_dn)


def _attn_body(nb, step, sink_ref, q_ref, z_ref, kv_ref, rope_ref, o_ref):
    n = step % nb
    blk = ATT_BLOCK
    group = ATT_HEADS // ATT_KV_HEADS
    pre = {}

    def prepare():
        back = jnp.minimum(n, 1)
        rows_c = pl.ds(pl.multiple_of(step * blk, blk), blk)
        rows_p = pl.ds(pl.multiple_of((step - back) * blk, blk), blk)
        pos_c = pl.ds(pl.multiple_of(n * blk, blk), blk)
        pos_p = pl.ds(pl.multiple_of((n - back) * blk, blk), blk)
        cos_c, up_c, dn_c = rope_ref[0, pos_c, :], rope_ref[1, pos_c, :], rope_ref[2, pos_c, :]
        cos_p, up_p, dn_p = rope_ref[0, pos_p, :], rope_ref[1, pos_p, :], rope_ref[2, pos_p, :]
        k_all = jnp.concatenate([
            _rope(kv_ref[rows_p, :LANES].astype(f32), cos_p, up_p, dn_p),
            _rope(kv_ref[rows_c, :LANES].astype(f32), cos_c, up_c, dn_c)], axis=0)
        v_all = jnp.concatenate([kv_ref[rows_p, LANES:].astype(f32),
                                 kv_ref[rows_c, LANES:].astype(f32)], axis=0)
        lane = lax.broadcasted_iota(jnp.int32, (2 * blk, LANES), 1)
        low = lane < HEAD_DIM
        k_sw = pltpu.roll(k_all, HEAD_DIM, 1)
        v_sw = pltpu.roll(v_all, HEAD_DIM, 1)
        pre["k"] = [[jnp.where(low, k_all, 0.0).astype(bf16),
                     jnp.where(low, 0.0, k_sw).astype(bf16)],
                    [jnp.where(low, k_sw, 0.0).astype(bf16),
                     jnp.where(low, 0.0, k_all).astype(bf16)]]
        pre["v"] = [[jnp.where(low, v_all, 0.0).astype(bf16),
                     jnp.where(low, 0.0, v_sw).astype(bf16)],
                    [jnp.where(low, v_sw, 0.0).astype(bf16),
                     jnp.where(low, 0.0, v_all).astype(bf16)]]
        qi = lax.broadcasted_iota(jnp.int32, (blk, blk), 0)
        kj = lax.broadcasted_iota(jnp.int32, (blk, blk), 1)
        pre["in_cur"] = kj <= qi
        pre["prev_bias"] = jnp.where(pre["in_cur"] | (n > 0), 0.0, NEG_INF)
        scale = HEAD_DIM ** -0.5
        pre["rope_q"] = (cos_c * scale, up_c * scale, dn_c * scale)

    def staged(tile_lo, tile_hi):
        heads = range(2 * tile_lo, 2 * tile_hi)
        tiles = {j: slice(j * LANES, (j + 1) * LANES) for j in range(tile_lo, tile_hi)}
        st = {}

        def scores():
            qs = {j: _rope(q_ref[:, cs].astype(f32), *pre["rope_q"]).astype(bf16)
                  for j, cs in tiles.items()}
            s2 = {hd: lax.dot_general(qs[hd // 2], pre["k"][hd // group][hd % 2],
                                      (((1,), (1,)), ((), ())), preferred_element_type=f32)
                  for hd in heads}
            st["s"] = {hd: jnp.where(pre["in_cur"], s2[hd][:, blk:], s2[hd][:, :blk])
                       + pre["prev_bias"] for hd in heads}

        def row_max():
            st["m"] = {hd: jnp.maximum(jnp.max(st["s"][hd], axis=-1, keepdims=True), sink_ref[hd])
                       for hd in heads}

        def exponent():
            st["p"] = {hd: jnp.exp(st["s"][hd] - st["m"][hd]) for hd in heads}
            st["den"] = {hd: jnp.sum(st["p"][hd], axis=-1, keepdims=True)
                         + jnp.exp(sink_ref[hd] - st["m"][hd]) for hd in heads}

        def values():
            p, in_cur = st["p"], pre["in_cur"]
            p2 = {hd: jnp.concatenate([jnp.where(in_cur, 0.0, p[hd]),
                                       jnp.where(in_cur, p[hd], 0.0)], axis=1).astype(bf16)
                  for hd in heads}
            st["pv"] = {hd: jnp.dot(p2[hd], pre["v"][hd // group][hd % 2],
                                    preferred_element_type=f32) * (1.0 / st["den"][hd])
                        for hd in heads}

        def output():
            pv = st["pv"]
            for j, cs in tiles.items():
                o_ref[:, cs] = ((pv[2 * j] + pv[2 * j + 1])
                                * z_ref[:, cs].astype(f32)).astype(o_ref.dtype)

        return [scores, row_max, exponent, values, output]

    n_tiles = ATT_HEADS // 2
    bounds = [0, 3, 6, n_tiles]
    return [prepare] + [stage for lo, hi in zip(bounds[:-1], bounds[1:])
                        for stage in staged(lo, hi)]


def _rope_tables(seq):
    half = ROT_DIM // 2
    inv_freq = ROPE_THETA ** (-jnp.arange(0, ROT_DIM, 2, dtype=f32) / ROT_DIM)
    ang = jnp.arange(seq, dtype=f32)[:, None] * inv_freq[None, :]
    cos, sin = jnp.cos(ang), jnp.sin(ang)
    ones = jnp.ones((seq, HEAD_DIM - ROT_DIM), f32)
    zeros = jnp.zeros((seq, HEAD_DIM - ROT_DIM), f32)
    zh = jnp.zeros((seq, half), f32)
    cos_t = jnp.concatenate([cos, cos, ones], axis=1)
    up_t = jnp.concatenate([-sin, zh, zeros], axis=1)
    dn_t = jnp.concatenate([zh, sin, zeros], axis=1)
    tab = jnp.stack([cos_t, up_t, dn_t])
    return jnp.concatenate([tab, tab], axis=2)


def _attn_job(p, kv, sinks, rope, bsz, seq):
    nb = seq // ATT_BLOCK
    w = ATT_WIDTH
    return _SideJob(
        steps=bsz * nb,
        body=functools.partial(_attn_body, nb),
        inputs=(sinks, p, p, kv, rope),
        in_specs=(None,
                  ((ATT_BLOCK, w), lambda step: (step, 3)),
                  ((ATT_BLOCK, w), lambda step: (step, 4)),
                  (kv.shape, lambda step: (0, 0)),
                  (rope.shape, lambda step: (0, 0, 0))),
        out_shapes=(jax.ShapeDtypeStruct((bsz * seq, w), bf16),),
        out_specs=(((ATT_BLOCK, w), lambda step: (step, 0)),),
        scratch=())


def _merge_kernel(ya_ref, yb_ref, yc_ref, g_ref, x_ref, wa_ref, wb_ref, wc_ref, wo_ref,
                  nw_ref, *out_refs, last, chunk):
    merged_ref = out_refs[-1]
    ys = (ya_ref, yb_ref, yc_ref)
    ws = (wa_ref, wb_ref, wc_ref)
    for c in range(D_MODEL // chunk):
        cs = slice(c * chunk, (c + 1) * chunk)
        acc = None
        for k in range(N_BRANCH):
            gate = jax.nn.sigmoid(g_ref[:, k * D_MODEL + c * chunk:k * D_MODEL + (c + 1) * chunk]
                                  .astype(f32))
            term = gate * jnp.dot(ys[k][...], ws[k][:, cs], preferred_element_type=f32)
            acc = term if acc is None else acc + term
        merged_ref[:, cs] = acc.astype(bf16)
    x_new = x_ref[...] + jnp.dot(merged_ref[...], wo_ref[...], preferred_element_type=f32)
    ms = jnp.mean(x_new * x_new, axis=-1, keepdims=True)
    normed = x_new * lax.rsqrt(ms + EPS) * nw_ref[...]
    if last:
        out_refs[0][...] = normed
    else:
        out_refs[0][...] = x_new
        out_refs[1][...] = normed.astype(bf16)


def _merge_out(ya_tm, yb, yc, gates, x, wa, wb, wc, wo, next_norm_w, bsz, seq, last,
               tm=MERGE_ROWS):
    t = bsz * seq
    nt = seq // tm
    d = D_MODEL
    row = lambda b, n: (b * nt + n, 0)
    const = lambda b, n: (0, 0)
    once = pl.Buffered(1)
    wspec = lambda m: pl.BlockSpec(m.shape, const, pipeline_mode=once)
    if last:
        out_shape = [jax.ShapeDtypeStruct((t, d), f32)]
        out_specs = [pl.BlockSpec((tm, d), row)]
    else:
        out_shape = [jax.ShapeDtypeStruct((t, d), f32), jax.ShapeDtypeStruct((t, d), bf16)]
        out_specs = [pl.BlockSpec((tm, d), row), pl.BlockSpec((tm, d), row)]
    return pl.pallas_call(
        functools.partial(_merge_kernel, last=last, chunk=MERGE_CHUNK),
        grid=(bsz, nt),
        in_specs=[pl.BlockSpec((tm, SSM_WIDTH), lambda b, n: (n, b)),
                  pl.BlockSpec((tm, SG_WIDTH), row),
                  pl.BlockSpec((tm, ATT_WIDTH), row),
                  pl.BlockSpec((tm, N_BRANCH * d), row),
                  pl.BlockSpec((tm, d), row),
                  wspec(wa), wspec(wb), wspec(wc), wspec(wo),
                  pl.BlockSpec((1, d), const)],
        out_specs=out_specs,
        out_shape=out_shape,
        scratch_shapes=[pltpu.VMEM((tm, d), bf16)],
        compiler_params=_cparams("parallel", "parallel"),
        name="merge_out",
    )(ya_tm, yb, yc, gates, x, wa, wb, wc, wo, next_norm_w.reshape(1, d))


def kernel(x, norm_w, w_in, ssm_a_re, ssm_a_im, ssm_log_dt, ssm_b_re, ssm_b_im, ssm_c_re, ssm_c_im, ssm_d, ssm_glu_w, ssm_glu_b, sg_ln_w, sg_ln_b, sg_w, sg_b, attn_sinks, w_branch_a, w_branch_b, w_branch_c, w_out, final_norm_w):
    bsz, seq, d = x.shape
    assert 2 * bsz == SUBLANES, "the S5 kernel packs two timesteps of all batches per sublane tile"
    depth = norm_w.shape[0]
    t = bsz * seq
    xf = x.reshape(t, d)
    rope = _rope_tables(seq)

    o_ub = 2 * SSM_WIDTH
    o_q = o_ub + 3 * SG_WIDTH
    o_k = o_q + ATT_WIDTH
    o_zc = o_k + 2 * KV_WIDTH
    o_g = o_zc + ATT_WIDTH

    tn = PROJ_COLS
    gate_tn = GATE_COLS
    s5_cols = [0, SSM_WIDTH]
    bc_cols = [o_ub, o_ub + tn, o_ub + 2 * tn, o_q, o_zc]
    gate_cols = [o_g + c * gate_tn for c in range(N_BRANCH * d // gate_tn)]

    b4s, lam2s, c2s = jax.vmap(_s5_params)(ssm_a_re, ssm_a_im, ssm_log_dt, ssm_b_re,
                                           ssm_b_im, ssm_c_re, ssm_c_im)
    layer_weights = (ssm_glu_w, w_branch_a, w_branch_b, w_branch_c, w_out)

    h = _rmsnorm(xf, norm_w[0], bf16)
    out = None
    for l in range(depth):
        p_bc, gw, wa, wb, wc, wo = _project(
            h, w_in, l, bc_cols, tn,
            side=_cast_job(layer_weights, l, steps=len(bc_cols) * (t // PROJ_ROWS)),
            activations=["gelu", "gelu", "silu", None, "silu"], name="proj_bc")
        kv = _project(h, w_in, l, [o_k], 2 * KV_WIDTH, name="proj_kv")
        uz, yb = _project(h, w_in, l, s5_cols, tn, time_major_batches=bsz,
                          side=_gmlp_job(p_bc, sg_ln_w[l], sg_ln_b[l], sg_w[l], sg_b[l]),
                          activations=[None, "silu"], name="proj_s5_gmlp")
        gates, yc = _project(h, w_in, l, gate_cols, gate_tn,
                             side=_attn_job(p_bc, kv, attn_sinks[l], rope, bsz, seq),
                             name="proj_gates_attn")
        ya = _s5_mixer(uz, b4s[l], lam2s[l], c2s[l], ssm_d[l], gw, ssm_glu_b[l], bsz)

        last = l == depth - 1
        next_w = final_norm_w if last else norm_w[l + 1]
        res = _merge_out(ya, yb, yc, gates, xf, wa, wb, wc, wo, next_w, bsz, seq, last)
        if last:
            out = res[0]
        else:
            xf, h = res
    return out.reshape(bsz, seq, d)
```
